```python
import math
import jax, jax.numpy as jnp
from jax import lax
import numpy as np

D_MODEL = 1024
BATCH = 8
SEQ = 2048
DEPTH = 4

MIX_A = 512
SGU_GROUPS = 4
SGU_DG = MIX_A // SGU_GROUPS
CHUNK = 128
N_Q_HEADS = 8
N_KV_HEADS = 2
HEAD_DIM = 64
GQ = N_Q_HEADS // N_KV_HEADS
MIX_B = N_Q_HEADS * HEAD_DIM
MIX_WIDTH = MIX_A + MIX_B
WINDOW = 128
BLOCK = 128
N_BUCKETS = 32
MAX_DISTANCE = 128
D_FF = 2816
IN_COLS = 2 * MIX_A + MIX_B + 2 * N_KV_HEADS * HEAD_DIM
EPS = 1e-6
NEG_INF = -1e30

kernel_name = "hybrid_sgu_swa_macaron_encoder"


def _rmsnorm(x, g):
    x32 = x.astype(jnp.float32)
    inv = lax.rsqrt(jnp.mean(x32 * x32, axis=-1, keepdims=True) + EPS)
    return (x32 * inv).astype(x.dtype) * g


def _swiglu(h, w_in, w_out):
    gu = h @ w_in
    gate, up = jnp.split(gu, 2, axis=-1)
    return (jax.nn.silu(gate) * up) @ w_out


def _t5_bucket(rel):
    nb = N_BUCKETS // 2
    ret = (rel > 0).astype(np.int32) * nb
    n = np.abs(rel)
    max_exact = nb // 2
    large = max_exact + (np.log(np.maximum(n, 1).astype(np.float32) / max_exact)
                         / math.log(MAX_DISTANCE / max_exact) * (nb - max_exact)).astype(np.int32)
    large = np.minimum(large, nb - 1)
    return ret + np.where(n < max_exact, n, large).astype(np.int32)


def _spatial_gating(a_proj, norm_g, w_s, b_s):
    B, S, _ = a_proj.shape
    z = jax.nn.gelu(a_proj)
    u, v = jnp.split(z, 2, axis=-1)
    v = _rmsnorm(v.reshape(B, S, SGU_GROUPS, SGU_DG), norm_g.reshape(SGU_GROUPS, SGU_DG))
    vc = v.reshape(B, S // CHUNK, CHUNK, SGU_GROUPS, SGU_DG)
    s = jnp.einsum('gpq,bcqgd->bcpgd', w_s, vc) + b_s.T[:, :, None]
    return u * s.reshape(B, S, MIX_A)


def _windowed_gqa(q, k, v, q_g, k_g, sink, rel_bias):
    B, S, _ = q.shape
    nB = S // BLOCK
    q = _rmsnorm(q.reshape(B, S, N_Q_HEADS, HEAD_DIM), q_g)
    k = _rmsnorm(k.reshape(B, S, N_KV_HEADS, HEAD_DIM), k_g)
    v = v.reshape(B, S, N_KV_HEADS, HEAD_DIM)
    pad = ((0, 0), (BLOCK, BLOCK), (0, 0), (0, 0))
    kp = jnp.pad(k, pad).reshape(B, nB + 2, BLOCK, N_KV_HEADS, HEAD_DIM)
    vp = jnp.pad(v, pad).reshape(B, nB + 2, BLOCK, N_KV_HEADS, HEAD_DIM)
    kw = jnp.concatenate([kp[:, :-2], kp[:, 1:-1], kp[:, 2:]], axis=2)
    vw = jnp.concatenate([vp[:, :-2], vp[:, 1:-1], vp[:, 2:]], axis=2)
    qb = q.reshape(B, nB, BLOCK, N_KV_HEADS, GQ, HEAD_DIM)
    s = jnp.einsum('bnqhgd,bnkhd->bnhgqk', qb, kw).astype(jnp.float32) * (HEAD_DIM ** -0.5)
    rel = np.arange(3 * BLOCK)[None, :] - BLOCK - np.arange(BLOCK)[:, None]
    bias = rel_bias[_t5_bucket(rel)]
    bias = jnp.transpose(bias, (2, 0, 1)).reshape(N_KV_HEADS, GQ, BLOCK, 3 * BLOCK).astype(jnp.float32)
    key_pos = np.arange(nB)[:, None] * BLOCK + np.arange(3 * BLOCK)[None, :] - BLOCK
    valid = (np.abs(rel) <= WINDOW)[None] & ((key_pos >= 0) & (key_pos < S))[:, None, :]
    s = jnp.where(jnp.asarray(valid)[None, :, None, None], s + bias, NEG_INF)
    sink_l = sink.astype(jnp.float32).reshape(N_KV_HEADS, GQ)[None, None, :, :, None, None]
    m = jnp.maximum(jnp.max(s, axis=-1, keepdims=True), sink_l)
    p = jnp.exp(s - m)
    p = p / (jnp.sum(p, axis=-1, keepdims=True) + jnp.exp(sink_l - m))
    o = jnp.einsum('bnhgqk,bnkhd->bnqhgd', p.astype(vw.dtype), vw)
    return o.reshape(B, S, MIX_B)


def setup_inputs(seed: int = 0) -> dict:
    key = jax.random.key(seed)
    ks = jax.random.split(key, 20)
    f = jnp.float32
    nrm = lambda k, shape, scale: jax.random.normal(k, shape, f) * scale
    gain = lambda k, shape: 1.0 + 0.05 * jax.random.normal(k, shape, f)
    return {
        "x": jax.random.normal(ks[0], (BATCH, SEQ, D_MODEL), f),
        "ffn1_norm": gain(ks[1], (DEPTH, D_MODEL)),
        "ffn1_w_in": nrm(ks[2], (DEPTH, D_MODEL, 2 * D_FF), D_MODEL ** -0.5),
        "ffn1_w_out": nrm(ks[3], (DEPTH, D_FF, D_MODEL), D_FF ** -0.5),
        "mix_norm": gain(ks[4], (DEPTH, D_MODEL)),
        "w_mix_in": nrm(ks[5], (DEPTH, D_MODEL, IN_COLS), D_MODEL ** -0.5),
        "sgu_norm": gain(ks[6], (DEPTH, MIX_A)),
        "sgu_w": nrm(ks[7], (DEPTH, SGU_GROUPS, CHUNK, CHUNK), 0.5 * CHUNK ** -0.5),
        "sgu_b": gain(ks[8], (DEPTH, SGU_GROUPS, CHUNK)),
        "q_norm": gain(ks[9], (DEPTH, HEAD_DIM)),
        "k_norm": gain(ks[10], (DEPTH, HEAD_DIM)),
        "sink": nrm(ks[11], (DEPTH, N_Q_HEADS), 0.5),
        "rel_bias": nrm(ks[12], (N_BUCKETS, N_Q_HEADS), 0.5),
        "w_mix_out": nrm(ks[13], (DEPTH, MIX_WIDTH, D_MODEL), MIX_WIDTH ** -0.5),
        "ffn2_norm": gain(ks[14], (DEPTH, D_MODEL)),
        "ffn2_w_in": nrm(ks[15], (DEPTH, D_MODEL, 2 * D_FF), D_MODEL ** -0.5),
        "ffn2_w_out": nrm(ks[16], (DEPTH, D_FF, D_MODEL), D_FF ** -0.5),
    }


def reference(x, ffn1_norm, ffn1_w_in, ffn1_w_out, mix_norm, w_mix_in, sgu_norm, sgu_w, sgu_b,
              q_norm, k_norm, sink, rel_bias, w_mix_out, ffn2_norm, ffn2_w_in, ffn2_w_out):
    for l in range(DEPTH):
        x = x + 0.5 * _swiglu(_rmsnorm(x, ffn1_norm[l]), ffn1_w_in[l], ffn1_w_out[l])
        h = _rmsnorm(x, mix_norm[l])
        p = h @ w_mix_in[l]
        a_proj = p[..., :2 * MIX_A]
        q = p[..., 2 * MIX_A:2 * MIX_A + MIX_B]
        kv_w = N_KV_HEADS * HEAD_DIM
        k = p[..., 2 * MIX_A + MIX_B:2 * MIX_A + MIX_B + kv_w]
        v = p[..., 2 * MIX_A + MIX_B + kv_w:]
        out_a = _spatial_gating(a_proj, sgu_norm[l], sgu_w[l], sgu_b[l])
        out_b = _windowed_gqa(q, k, v, q_norm[l], k_norm[l], sink[l], rel_bias)
        x = x + jnp.concatenate([out_a, out_b], axis=-1) @ w_mix_out[l]
        x = x + 0.5 * _swiglu(_rmsnorm(x, ffn2_norm[l]), ffn2_w_in[l], ffn2_w_out[l])
    return x
```

```python
import functools
import math

import jax
import jax.numpy as jnp
import numpy as np
from jax import lax
from jax.experimental import pallas as pl
from jax.experimental.pallas import tpu as pltpu

F32 = jnp.float32
BF16 = jnp.bfloat16

EPS = 1e-6
NEG_INF = -1e30

LANES = 128
MXU_DIM = 256
VMEM_LIMIT_BYTES = 56 * 1024 * 1024

SGU_GROUPS = 4
CHUNK = 128
N_Q_HEADS = 8
N_KV_HEADS = 2
HEAD_DIM = 64
N_BUCKETS = 32
MAX_DISTANCE = 128
MIX_A = SGU_GROUPS * LANES
MIX_B = N_Q_HEADS * HEAD_DIM
KV_W = N_KV_HEADS * HEAD_DIM
N_VARIANTS = 4

FFN_ROWS = 512
FFN_CHUNKS = 2
MIX_IN_ROWS = 512
MIX_OUT_ROWS = 512


def _rms_scale(x, width):
    return lax.rsqrt(jnp.sum(x * x, axis=-1, keepdims=True) * (1.0 / width) + EPS)


def _ffn_kernel(x_ref, g_ref, win_ref, wout_ref, o_ref, *, d_ff):
    x = x_ref[...]
    h = ((x * _rms_scale(x, x.shape[-1])) * g_ref[...]).astype(BF16)
    tf = d_ff // FFN_CHUNKS
    y = None
    for c in range(FFN_CHUNKS):
        lo = c * tf
        gate = jnp.dot(h, win_ref[:, lo:lo + tf], preferred_element_type=F32)
        up = jnp.dot(h, win_ref[:, d_ff + lo:d_ff + lo + tf], preferred_element_type=F32)
        a = (gate * jax.nn.sigmoid(gate) * up).astype(BF16)
        part = jnp.dot(a, wout_ref[lo:lo + tf, :], preferred_element_type=F32)
        y = part if y is None else y + part
    o_ref[...] = x + 0.5 * y


def _ffn(x, gain, w_in, w_out, layer):
    t, d = x.shape
    d_ff = w_out.shape[1]
    resident = pl.Buffered(1)
    return pl.pallas_call(
        functools.partial(_ffn_kernel, d_ff=d_ff),
        grid=(t // FFN_ROWS,),
        in_specs=[
            pl.BlockSpec((FFN_ROWS, d), lambda i: (i, 0)),
            pl.BlockSpec((None, 1, d), lambda i: (layer, 0, 0)),
            pl.BlockSpec((None, d, 2 * d_ff), lambda i: (layer, 0, 0), pipeline_mode=resident),
            pl.BlockSpec((None, d_ff, d), lambda i: (layer, 0, 0), pipeline_mode=resident),
        ],
        out_specs=pl.BlockSpec((FFN_ROWS, d), lambda i: (i, 0)),
        out_shape=jax.ShapeDtypeStruct((t, d), F32),
        compiler_params=pltpu.CompilerParams(
            dimension_semantics=("parallel",), vmem_limit_bytes=VMEM_LIMIT_BYTES),
        name="ffn",
    )(x, gain, w_in, w_out)


def _segment_sumsq(x, ones_ref):
    x2 = x * x
    hi = x2.astype(BF16)
    lo = (x2 - hi.astype(F32)).astype(BF16)
    w = x.shape[-1]
    step = min(w, MXU_DIM)
    ones = ones_ref[0:step, 0:step]
    parts = []
    for c in range(0, w, step):
        parts.append(jnp.dot(hi[:, c:c + step], ones, preferred_element_type=F32)
                     + jnp.dot(lo[:, c:c + step], ones, preferred_element_type=F32))
    return parts[0] if len(parts) == 1 else jnp.concatenate(parts, axis=-1)


def _lane_half_variants(x):
    swapped = pltpu.roll(x, HEAD_DIM, axis=1)
    low = lax.broadcasted_iota(jnp.int32, x.shape, 1) < HEAD_DIM
    zero = jnp.zeros_like(x)
    return jnp.concatenate([
        jnp.where(low, x, zero),
        jnp.where(low, zero, swapped),
        jnp.where(low, swapped, zero),
        jnp.where(low, zero, x),
    ], axis=-1)


def _mix_in_kernel(x_ref, g_ref, win_ref, sgu_g_ref, sgu_w_ref, sgu_b_ref, qg_ref, kg_ref,
                   ones_ref, outa_ref, q_ref, k4_ref, v4_ref):
    x = x_ref[...]
    rows = x.shape[0]
    h = ((x * _rms_scale(x, x.shape[-1])) * g_ref[...]).astype(BF16)
    p = jnp.dot(h, win_ref[...], preferred_element_type=F32)

    z = jax.nn.gelu(p[:, :2 * MIX_A])
    for g in range(SGU_GROUPS):
        u = z[:, g * LANES:(g + 1) * LANES]
        v = z[:, MIX_A + g * LANES:MIX_A + (g + 1) * LANES]
        vn = ((v * _rms_scale(v, LANES)) * sgu_g_ref[:, g * LANES:(g + 1) * LANES]).astype(BF16)
        w_s = sgu_w_ref[g]
        b_s = sgu_b_ref[g]
        for c in range(rows // CHUNK):
            r = slice(c * CHUNK, (c + 1) * CHUNK)
            s = jnp.dot(w_s, vn[r], preferred_element_type=F32) + b_s
            outa_ref[r, g * LANES:(g + 1) * LANES] = (u[r] * s).astype(BF16)

    q = p[:, 2 * MIX_A:2 * MIX_A + MIX_B]
    q_inv = lax.rsqrt(_segment_sumsq(q, ones_ref) * (1.0 / HEAD_DIM) + EPS)
    q_ref[...] = (((q * q_inv) * qg_ref[...]) * (HEAD_DIM ** -0.5)).astype(BF16)

    k = p[:, 2 * MIX_A + MIX_B:2 * MIX_A + MIX_B + KV_W]
    k_inv = lax.rsqrt(_segment_sumsq(k, ones_ref) * (1.0 / HEAD_DIM) + EPS)
    k4_ref[...] = _lane_half_variants((k * k_inv) * kg_ref[...]).astype(BF16)

    v_attn = p[:, 2 * MIX_A + MIX_B + KV_W:]
    v4_ref[...] = _lane_half_variants(v_attn).astype(BF16)


def _mix_in(x, gain, w_in, sgu_g, sgu_w, sgu_b, q_g, k_g, ones_bd, layer):
    t, d = x.shape
    in_cols = w_in.shape[-1]
    rows = MIX_IN_ROWS
    row_spec = lambda w: pl.BlockSpec((rows, w), lambda i: (i, 0))
    layer_spec = lambda *shape: pl.BlockSpec((None,) + shape, lambda i: (layer,) + (0,) * len(shape))
    return pl.pallas_call(
        _mix_in_kernel,
        grid=(t // rows,),
        in_specs=[
            row_spec(d),
            layer_spec(1, d),
            layer_spec(d, in_cols),
            layer_spec(1, MIX_A),
            layer_spec(SGU_GROUPS, CHUNK, CHUNK),
            layer_spec(SGU_GROUPS, CHUNK, LANES),
            layer_spec(1, MIX_B),
            layer_spec(1, KV_W),
            pl.BlockSpec((MXU_DIM, MXU_DIM), lambda i: (0, 0)),
        ],
        out_specs=[row_spec(MIX_A), row_spec(MIX_B), row_spec(N_VARIANTS * LANES),
                   row_spec(N_VARIANTS * LANES)],
        out_shape=[
            jax.ShapeDtypeStruct((t, MIX_A), BF16),
            jax.ShapeDtypeStruct((t, MIX_B), BF16),
            jax.ShapeDtypeStruct((t, N_VARIANTS * LANES), BF16),
            jax.ShapeDtypeStruct((t, N_VARIANTS * LANES), BF16),
        ],
        compiler_params=pltpu.CompilerParams(
            dimension_semantics=("parallel",), vmem_limit_bytes=VMEM_LIMIT_BYTES),
        name="mix_in",
    )(x, gain, w_in, sgu_g, sgu_w, sgu_b, q_g, k_g, ones_bd)


def _mix_out_kernel(sink_ref, x_ref, outa_ref, q_ref, k4_ref, v4_ref, bias_ref, wout_ref,
                    o_ref, kpad_ref, vpad_ref, attn_ref, *, layer, n_blocks):
    j = pl.program_id(1)
    seq = k4_ref.shape[0]

    @pl.when(j == 0)
    def _():
        zeros = jnp.zeros((CHUNK, N_VARIANTS * LANES), BF16)
        for pad_ref, src_ref in ((kpad_ref, k4_ref), (vpad_ref, v4_ref)):
            pad_ref[0:CHUNK, :] = zeros
            pad_ref[CHUNK:CHUNK + seq, :] = src_ref[...]
            pad_ref[CHUNK + seq:, :] = zeros

    blocks_per_step = x_ref.shape[0] // CHUNK
    first_rows = lax.broadcasted_iota(jnp.int32, (2 * CHUNK, 1), 0) < CHUNK
    for i in range(blocks_per_step):
        n = j * blocks_per_step + i
        edge = jnp.where(n == 0, 0, jnp.where(n == n_blocks - 1, 2, 1))
        win = pl.ds(pl.multiple_of(n * CHUNK, CHUNK), 3 * CHUNK)
        r = slice(i * CHUNK, (i + 1) * CHUNK)
        for kvh in range(N_KV_HEADS):
            cg = 2 * kvh
            lhs = jnp.concatenate([q_ref[r, cg * LANES:(cg + 1) * LANES],
                                   q_ref[r, (cg + 1) * LANES:(cg + 2) * LANES]], axis=0)
            o_kv = None
            for half in range(2):
                vi = 2 * kvh + half
                cols = slice(vi * LANES, (vi + 1) * LANES)
                head_a = 4 * kvh + half
                head_b = head_a + 2
                sink = jnp.where(first_rows, sink_ref[layer, head_a], sink_ref[layer, head_b])
                s = lax.dot_general(lhs, kpad_ref[win, cols], (((1,), (1,)), ((), ())),
                                    preferred_element_type=F32)
                s = s + bias_ref[edge, vi]
                m = jnp.maximum(jnp.max(s, axis=-1, keepdims=True), sink)
                e = jnp.exp(s - m)
                denom = jnp.sum(e, axis=-1, keepdims=True) + jnp.exp(sink - m)
                pv = jnp.dot(e.astype(BF16), vpad_ref[win, cols], preferred_element_type=F32)
                pv = pv * (1.0 / denom)
                o_kv = pv if o_kv is None else o_kv + pv
            attn_ref[r, cg * LANES:(cg + 1) * LANES] = o_kv[0:CHUNK].astype(BF16)
            attn_ref[r, (cg + 1) * LANES:(cg + 2) * LANES] = o_kv[CHUNK:].astype(BF16)

    y = (jnp.dot(outa_ref[...], wout_ref[0:MIX_A, :], preferred_element_type=F32)
         + jnp.dot(attn_ref[...], wout_ref[MIX_A:, :], preferred_element_type=F32))
    o_ref[...] = x_ref[...] + y


def _mix_out(x, outa, q, k4, v4, sink, bias_tab, w_out, layer, batch):
    t, d = x.shape
    seq = t // batch
    rows = MIX_OUT_ROWS
    steps = seq // rows
    row_spec = lambda w: pl.BlockSpec((rows, w), lambda b, j: (b * steps + j, 0))
    seq_spec = pl.BlockSpec((seq, N_VARIANTS * LANES), lambda b, j: (b, 0))
    return pl.pallas_call(
        functools.partial(_mix_out_kernel, layer=layer, n_blocks=seq // CHUNK),
        grid=(batch, steps),
        in_specs=[
            pl.BlockSpec(memory_space=pltpu.SMEM),
            row_spec(d), row_spec(MIX_A), row_spec(MIX_B), seq_spec, seq_spec,
            pl.BlockSpec(bias_tab.shape, lambda b, j: (0, 0, 0, 0)),
            pl.BlockSpec((None, MIX_A + MIX_B, d), lambda b, j: (layer, 0, 0)),
        ],
        out_specs=row_spec(d),
        out_shape=jax.ShapeDtypeStruct((t, d), F32),
        scratch_shapes=[
            pltpu.VMEM((seq + 2 * CHUNK, N_VARIANTS * LANES), BF16),
            pltpu.VMEM((seq + 2 * CHUNK, N_VARIANTS * LANES), BF16),
            pltpu.VMEM((rows, MIX_B), BF16),
        ],
        compiler_params=pltpu.CompilerParams(
            dimension_semantics=("parallel", "arbitrary"), vmem_limit_bytes=VMEM_LIMIT_BYTES),
        name="mix_out",
    )(sink, x, outa, q, k4, v4, bias_tab, w_out)


def _t5_bucket(rel):
    nb = N_BUCKETS // 2
    ret = (rel > 0).astype(np.int32) * nb
    n = np.abs(rel)
    max_exact = nb // 2
    large = max_exact + (np.log(np.maximum(n, 1).astype(np.float32) / max_exact)
                         / math.log(MAX_DISTANCE / max_exact) * (nb - max_exact)).astype(np.int32)
    large = np.minimum(large, nb - 1)
    return ret + np.where(n < max_exact, n, large).astype(np.int32)


def _bias_table(rel_bias):
    rel = np.arange(3 * CHUNK)[None, :] - CHUNK - np.arange(CHUNK)[:, None]
    bias = jnp.transpose(rel_bias.astype(F32)[_t5_bucket(rel)], (2, 0, 1))
    band = np.abs(rel) <= CHUNK
    col = np.arange(3 * CHUNK)[None, :]
    edge_masks = [band & (col >= CHUNK), band, band & (col < 2 * CHUNK)]
    head_rows = np.array([[4 * (vi // 2) + (vi % 2), 4 * (vi // 2) + (vi % 2) + 2]
                          for vi in range(N_VARIANTS)])
    per_variant = bias[head_rows.reshape(-1)].reshape(N_VARIANTS, 2 * CHUNK, 3 * CHUNK)
    tabs = [jnp.where(np.tile(mask, (2, 1))[None], per_variant, NEG_INF) for mask in edge_masks]
    return jnp.stack(tabs, axis=0)


def kernel(x, ffn1_norm, ffn1_w_in, ffn1_w_out, mix_norm, w_mix_in, sgu_norm, sgu_w, sgu_b,
           q_norm, k_norm, sink, rel_bias, w_mix_out, ffn2_norm, ffn2_w_in, ffn2_w_out):
    batch, seq, d = x.shape
    depth = ffn1_norm.shape[0]
    assert seq % MIX_OUT_ROWS == 0 and (batch * seq) % FFN_ROWS == 0

    row3 = lambda a: a.astype(F32)[:, None, :]
    ffn1_g, mix_g, ffn2_g, sgu_g = row3(ffn1_norm), row3(mix_norm), row3(ffn2_norm), row3(sgu_norm)
    q_g = row3(jnp.tile(q_norm, (1, N_Q_HEADS)))
    k_g = row3(jnp.tile(k_norm, (1, N_KV_HEADS)))
    sgu_b3 = jnp.broadcast_to(sgu_b.astype(F32)[..., None], sgu_b.shape + (LANES,))
    ones_bd = jnp.asarray(np.kron(np.eye(MXU_DIM // HEAD_DIM), np.ones((HEAD_DIM, HEAD_DIM))), BF16)
    bias_tab = _bias_table(rel_bias)
    sink = sink.astype(F32)
    bf = lambda w: w.astype(BF16)
    ffn1_wi, ffn1_wo, ffn2_wi, ffn2_wo = bf(ffn1_w_in), bf(ffn1_w_out), bf(ffn2_w_in), bf(ffn2_w_out)
    mix_wi, mix_wo, sgu_wb = bf(w_mix_in), bf(w_mix_out), bf(sgu_w)

    xt = x.reshape(batch * seq, d)
    for l in range(depth):
        xt = _ffn(xt, ffn1_g, ffn1_wi, ffn1_wo, l)
        outa, q, k4, v4 = _mix_in(xt, mix_g, mix_wi, sgu_g, sgu_wb, sgu_b3, q_g, k_g, ones_bd, l)
        xt = _mix_out(xt, outa, q, k4, v4, sink, bias_tab, mix_wo, l, batch)
        xt = _ffn(xt, ffn2_g, ffn2_wi, ffn2_wo, l)
    return xt.reshape(batch, seq, d)
```

```python
import functools
import math

import jax
import jax.numpy as jnp
import numpy as np
from jax import lax
from jax.experimental import pallas as pl
from jax.experimental.pallas import tpu as pltpu

F32 = jnp.float32
BF16 = jnp.bfloat16

EPS = 1e-6
NEG_INF = -1e30

LANES = 128
MXU_DIM = 256
VMEM_LIMIT_BYTES = 56 * 1024 * 1024

SGU_GROUPS = 4
CHUNK = 128
N_Q_HEADS = 8
N_KV_HEADS = 2
HEAD_DIM = 64
N_BUCKETS = 32
MAX_DISTANCE = 128
MIX_A = SGU_GROUPS * LANES
MIX_B = N_Q_HEADS * HEAD_DIM
KV_W = N_KV_HEADS * HEAD_DIM
N_VARIANTS = 4

FFN_ROWS = 512
FFN_CHUNK_TILES = (6, 5)
MIX_IN_ROWS = 512
MIX_OUT_ROWS = 512


def _rms_scale(x, width):
    return lax.rsqrt(jnp.sum(x * x, axis=-1, keepdims=True) * (1.0 / width) + EPS)


def _ffn_kernel(x_ref, g_ref, win_ref, wout_ref, o_ref, *, d_ff):
    x = x_ref[...]
    h = ((x * _rms_scale(x, x.shape[-1])) * g_ref[...]).astype(BF16)
    assert sum(FFN_CHUNK_TILES) * MXU_DIM == d_ff
    y = None
    lo = 0
    for tiles in FFN_CHUNK_TILES:
        tf = tiles * MXU_DIM
        gate = jnp.dot(h, win_ref[:, lo:lo + tf], preferred_element_type=F32)
        up = jnp.dot(h, win_ref[:, d_ff + lo:d_ff + lo + tf], preferred_element_type=F32)
        a = (gate * jax.nn.sigmoid(gate) * up).astype(BF16)
        part = jnp.dot(a, wout_ref[lo:lo + tf, :], preferred_element_type=F32)
        y = part if y is None else y + part
        lo += tf
    o_ref[...] = x + 0.5 * y


def _ffn(x, gain, w_in, w_out, layer):
    t, d = x.shape
    d_ff = w_out.shape[1]
    resident = pl.Buffered(1)
    return pl.pallas_call(
        functools.partial(_ffn_kernel, d_ff=d_ff),
        grid=(t // FFN_ROWS,),
        in_specs=[
            pl.BlockSpec((FFN_ROWS, d), lambda i: (i, 0)),
            pl.BlockSpec((None, 1, d), lambda i: (layer, 0, 0)),
            pl.BlockSpec((None, d, 2 * d_ff), lambda i: (layer, 0, 0), pipeline_mode=resident),
            pl.BlockSpec((None, d_ff, d), lambda i: (layer, 0, 0), pipeline_mode=resident),
        ],
        out_specs=pl.BlockSpec((FFN_ROWS, d), lambda i: (i, 0)),
        out_shape=jax.ShapeDtypeStruct((t, d), F32),
        compiler_params=pltpu.CompilerParams(
            dimension_semantics=("parallel",), vmem_limit_bytes=VMEM_LIMIT_BYTES),
        name="ffn",
    )(x, gain, w_in, w_out)


def _segment_sumsq(x, ones_ref):
    x2 = x * x
    hi = x2.astype(BF16)
    lo = (x2 - hi.astype(F32)).astype(BF16)
    w = x.shape[-1]
    step = min(w, MXU_DIM)
    ones = ones_ref[0:step, 0:step]
    parts = []
    for c in range(0, w, step):
        parts.append(jnp.dot(hi[:, c:c + step], ones, preferred_element_type=F32)
                     + jnp.dot(lo[:, c:c + step], ones, preferred_element_type=F32))
    return parts[0] if len(parts) == 1 else jnp.concatenate(parts, axis=-1)


def _lane_half_variants(x):
    swapped = pltpu.roll(x, HEAD_DIM, axis=1)
    low = lax.broadcasted_iota(jnp.int32, x.shape, 1) < HEAD_DIM
    zero = jnp.zeros_like(x)
    return jnp.concatenate([
        jnp.where(low, x, zero),
        jnp.where(low, zero, swapped),
        jnp.where(low, swapped, zero),
        jnp.where(low, zero, x),
    ], axis=-1)


def _mix_in_kernel(x_ref, g_ref, win_ref, sgu_g_ref, sgu_w_ref, sgu_b_ref, qg_ref, kg_ref,
                   ones_ref, outa_ref, q_ref, k4_ref, v4_ref):
    x = x_ref[...]
    rows = x.shape[0]
    h = ((x * _rms_scale(x, x.shape[-1])) * g_ref[...]).astype(BF16)
    p = jnp.dot(h, win_ref[...], preferred_element_type=F32)

    z = jax.nn.gelu(p[:, :2 * MIX_A])
    for g in range(SGU_GROUPS):
        u = z[:, g * LANES:(g + 1) * LANES]
        v = z[:, MIX_A + g * LANES:MIX_A + (g + 1) * LANES]
        vn = ((v * _rms_scale(v, LANES)) * sgu_g_ref[:, g * LANES:(g + 1) * LANES]).astype(BF16)
        w_s = sgu_w_ref[g]
        b_s = sgu_b_ref[g]
        for c in range(rows // CHUNK):
            r = slice(c * CHUNK, (c + 1) * CHUNK)
            s = jnp.dot(w_s, vn[r], preferred_element_type=F32) + b_s
            outa_ref[r, g * LANES:(g + 1) * LANES] = (u[r] * s).astype(BF16)

    q = p[:, 2 * MIX_A:2 * MIX_A + MIX_B]
    q_inv = lax.rsqrt(_segment_sumsq(q, ones_ref) * (1.0 / HEAD_DIM) + EPS)
    q_ref[...] = (((q * q_inv) * qg_ref[...]) * (HEAD_DIM ** -0.5)).astype(BF16)

    k = p[:, 2 * MIX_A + MIX_B:2 * MIX_A + MIX_B + KV_W]
    k_inv = lax.rsqrt(_segment_sumsq(k, ones_ref) * (1.0 / HEAD_DIM) + EPS)
    k4_ref[...] = _lane_half_variants((k * k_inv) * kg_ref[...]).astype(BF16)

    v_attn = p[:, 2 * MIX_A + MIX_B + KV_W:]
    v4_ref[...] = _lane_half_variants(v_attn).astype(BF16)


def _mix_in(x, gain, w_in, sgu_g, sgu_w, sgu_b, q_g, k_g, ones_bd, layer):
    t, d = x.shape
    in_cols = w_in.shape[-1]
    rows = MIX_IN_ROWS
    row_spec = lambda w: pl.BlockSpec((rows, w), lambda i: (i, 0))
    layer_spec = lambda *shape: pl.BlockSpec((None,) + shape, lambda i: (layer,) + (0,) * len(shape))
    return pl.pallas_call(
        _mix_in_kernel,
        grid=(t // rows,),
        in_specs=[
            row_spec(d),
            layer_spec(1, d),
            layer_spec(d, in_cols),
            layer_spec(1, MIX_A),
            layer_spec(SGU_GROUPS, CHUNK, CHUNK),
            layer_spec(SGU_GROUPS, CHUNK, LANES),
            layer_spec(1, MIX_B),
            layer_spec(1, KV_W),
            pl.BlockSpec((MXU_DIM, MXU_DIM), lambda i: (0, 0)),
        ],
        out_specs=[row_spec(MIX_A), row_spec(MIX_B), row_spec(N_VARIANTS * LANES),
                   row_spec(N_VARIANTS * LANES)],
        out_shape=[
            jax.ShapeDtypeStruct((t, MIX_A), BF16),
            jax.ShapeDtypeStruct((t, MIX_B), BF16),
            jax.ShapeDtypeStruct((t, N_VARIANTS * LANES), BF16),
            jax.ShapeDtypeStruct((t, N_VARIANTS * LANES), BF16),
        ],
        compiler_params=pltpu.CompilerParams(
            dimension_semantics=("parallel",), vmem_limit_bytes=VMEM_LIMIT_BYTES),
        name="mix_in",
    )(x, gain, w_in, sgu_g, sgu_w, sgu_b, q_g, k_g, ones_bd)


def _mix_out_kernel(sink_ref, x_ref, outa_ref, q_ref, k4_ref, v4_ref, bias_ref, wout_ref,
                    o_ref, kpad_ref, vpad_ref, attn_ref, *, layer, n_blocks):
    j = pl.program_id(1)
    seq = k4_ref.shape[0]

    @pl.when(j == 0)
    def _():
        zeros = jnp.zeros((CHUNK, N_VARIANTS * LANES), BF16)
        for pad_ref, src_ref in ((kpad_ref, k4_ref), (vpad_ref, v4_ref)):
            pad_ref[0:CHUNK, :] = zeros
            pad_ref[CHUNK:CHUNK + seq, :] = src_ref[...]
            pad_ref[CHUNK + seq:, :] = zeros

    blocks_per_step = x_ref.shape[0] // CHUNK
    first_rows = lax.broadcasted_iota(jnp.int32, (2 * CHUNK, 1), 0) < CHUNK
    for i in range(blocks_per_step):
        n = j * blocks_per_step + i
        edge = jnp.where(n == 0, 0, jnp.where(n == n_blocks - 1, 2, 1))
        win = pl.ds(pl.multiple_of(n * CHUNK, CHUNK), 3 * CHUNK)
        r = slice(i * CHUNK, (i + 1) * CHUNK)
        for kvh in range(N_KV_HEADS):
            cg = 2 * kvh
            lhs = jnp.concatenate([q_ref[r, cg * LANES:(cg + 1) * LANES],
                                   q_ref[r, (cg + 1) * LANES:(cg + 2) * LANES]], axis=0)
            o_kv = None
            for half in range(2):
                vi = 2 * kvh + half
                cols = slice(vi * LANES, (vi + 1) * LANES)
                head_a = 4 * kvh + half
                head_b = head_a + 2
                sink = jnp.where(first_rows, sink_ref[layer, head_a], sink_ref[layer, head_b])
                s = lax.dot_general(lhs, kpad_ref[win, cols], (((1,), (1,)), ((), ())),
                                    preferred_element_type=F32)
                s = s + bias_ref[edge, vi]
                m = jnp.maximum(jnp.max(s, axis=-1, keepdims=True), sink)
                e = jnp.exp(s - m)
                denom = jnp.sum(e, axis=-1, keepdims=True) + jnp.exp(sink - m)
                pv = jnp.dot(e.astype(BF16), vpad_ref[win, cols], preferred_element_type=F32)
                pv = pv * (1.0 / denom)
                o_kv = pv if o_kv is None else o_kv + pv
            attn_ref[r, cg * LANES:(cg + 1) * LANES] = o_kv[0:CHUNK].astype(BF16)
            attn_ref[r, (cg + 1) * LANES:(cg + 2) * LANES] = o_kv[CHUNK:].astype(BF16)

    y = (jnp.dot(outa_ref[...], wout_ref[0:MIX_A, :], preferred_element_type=F32)
         + jnp.dot(attn_ref[...], wout_ref[MIX_A:, :], preferred_element_type=F32))
    o_ref[...] = x_ref[...] + y


def _mix_out(x, outa, q, k4, v4, sink, bias_tab, w_out, layer, batch):
    t, d = x.shape
    seq = t // batch
    rows = MIX_OUT_ROWS
    steps = seq // rows
    row_spec = lambda w: pl.BlockSpec((rows, w), lambda b, j: (b * steps + j, 0))
    seq_spec = pl.BlockSpec((seq, N_VARIANTS * LANES), lambda b, j: (b, 0))
    return pl.pallas_call(
        functools.partial(_mix_out_kernel, layer=layer, n_blocks=seq // CHUNK),
        grid=(batch, steps),
        in_specs=[
            pl.BlockSpec(memory_space=pltpu.SMEM),
            row_spec(d), row_spec(MIX_A), row_spec(MIX_B), seq_spec, seq_spec,
            pl.BlockSpec(bias_tab.shape, lambda b, j: (0, 0, 0, 0)),
            pl.BlockSpec((None, MIX_A + MIX_B, d), lambda b, j: (layer, 0, 0)),
        ],
        out_specs=row_spec(d),
        out_shape=jax.ShapeDtypeStruct((t, d), F32),
        scratch_shapes=[
            pltpu.VMEM((seq + 2 * CHUNK, N_VARIANTS * LANES), BF16),
            pltpu.VMEM((seq + 2 * CHUNK, N_VARIANTS * LANES), BF16),
            pltpu.VMEM((rows, MIX_B), BF16),
        ],
        compiler_params=pltpu.CompilerParams(
            dimension_semantics=("parallel", "arbitrary"), vmem_limit_bytes=VMEM_LIMIT_BYTES),
        name="mix_out",
    )(sink, x, outa, q, k4, v4, bias_tab, w_out)


def _t5_bucket(rel):
    nb = N_BUCKETS // 2
    ret = (rel > 0).astype(np.int32) * nb
    n = np.abs(rel)
    max_exact = nb // 2
    large = max_exact + (np.log(np.maximum(n, 1).astype(np.float32) / max_exact)
                         / math.log(MAX_DISTANCE / max_exact) * (nb - max_exact)).astype(np.int32)
    large = np.minimum(large, nb - 1)
    return ret + np.where(n < max_exact, n, large).astype(np.int32)


def _bias_table(rel_bias):
    rel = np.arange(3 * CHUNK)[None, :] - CHUNK - np.arange(CHUNK)[:, None]
    bucket = _t5_bucket(rel)
    rb = rel_bias.astype(F32)
    bias = jnp.zeros((N_Q_HEADS,) + rel.shape, F32)
    for b in range(N_BUCKETS):
        bias = jnp.where((bucket == b)[None], rb[b][:, None, None], bias)
    band = np.abs(rel) <= CHUNK
    col = np.arange(3 * CHUNK)[None, :]
    edge_masks = [band & (col >= CHUNK), band, band & (col < 2 * CHUNK)]
    head_rows = np.array([[4 * (vi // 2) + (vi % 2), 4 * (vi // 2) + (vi % 2) + 2]
                          for vi in range(N_VARIANTS)])
    per_variant = jnp.stack([jnp.concatenate([bias[a], bias[b]], axis=0) for a, b in head_rows])
    tabs = [jnp.where(np.tile(mask, (2, 1))[None], per_variant, NEG_INF) for mask in edge_masks]
    return jnp.stack(tabs, axis=0)


def kernel(x, ffn1_norm, ffn1_w_in, ffn1_w_out, mix_norm, w_mix_in, sgu_norm, sgu_w, sgu_b,
           q_norm, k_norm, sink, rel_bias, w_mix_out, ffn2_norm, ffn2_w_in, ffn2_w_out):
    batch, seq, d = x.shape
    depth = ffn1_norm.shape[0]
    assert seq % MIX_OUT_ROWS == 0 and (batch * seq) % FFN_ROWS == 0

    row3 = lambda a: a.astype(F32)[:, None, :]
    ffn1_g, mix_g, ffn2_g, sgu_g = row3(ffn1_norm), row3(mix_norm), row3(ffn2_norm), row3(sgu_norm)
    q_g = row3(jnp.tile(q_norm, (1, N_Q_HEADS)))
    k_g = row3(jnp.tile(k_norm, (1, N_KV_HEADS)))
    sgu_b3 = jnp.broadcast_to(sgu_b.astype(F32)[..., None], sgu_b.shape + (LANES,))
    ones_bd = jnp.asarray(np.kron(np.eye(MXU_DIM // HEAD_DIM), np.ones((HEAD_DIM, HEAD_DIM))), BF16)
    bias_tab = _bias_table(rel_bias)
    sink = sink.astype(F32)
    bf = lambda w: w.astype(BF16)
    ffn1_wi, ffn1_wo, ffn2_wi, ffn2_wo = bf(ffn1_w_in), bf(ffn1_w_out), bf(ffn2_w_in), bf(ffn2_w_out)
    mix_wi, mix_wo, sgu_wb = bf(w_mix_in), bf(w_mix_out), bf(sgu_w)

    xt = x.reshape(batch * seq, d)
    for l in range(depth):
        xt = _ffn(xt, ffn1_g, ffn1_wi, ffn1_wo, l)
        outa, q, k4, v4 = _mix_in(xt, mix_g, mix_wi, sgu_g, sgu_wb, sgu_b3, q_g, k_g, ones_bd, l)
        xt = _mix_out(xt, outa, q, k4, v4, sink, bias_tab, mix_wo, l, batch)
        xt = _ffn(xt, ffn2_g, ffn2_wi, ffn2_wo, l)
    return xt.reshape(batch, seq, d)
```

```python
import functools
import math

import jax
import jax.numpy as jnp
import numpy as np
from jax import lax
from jax.experimental import pallas as pl
from jax.experimental.pallas import tpu as pltpu

F32 = jnp.float32
BF16 = jnp.bfloat16

EPS = 1e-6
NEG_INF = -1e30

LANES = 128
MXU_DIM = 256
VMEM_LIMIT_BYTES = 56 * 1024 * 1024

SGU_GROUPS = 4
CHUNK = 128
N_Q_HEADS = 8
N_KV_HEADS = 2
HEAD_DIM = 64
N_BUCKETS = 32
MAX_DISTANCE = 128
MIX_A = SGU_GROUPS * LANES
MIX_B = N_Q_HEADS * HEAD_DIM
KV_W = N_KV_HEADS * HEAD_DIM
N_VARIANTS = 4

FFN_ROWS = 512
FFN_CHUNK_TILES = (6, 5)
MIX_IN_ROWS = 512
MIX_OUT_ROWS = 512


def _rms_scale(x, width):
    return lax.rsqrt(jnp.sum(x * x, axis=-1, keepdims=True) * (1.0 / width) + EPS)


def _ffn_kernel(x_ref, g_ref, win_ref, wout_ref, o_ref, *, d_ff):
    x = x_ref[...]
    h = ((x * _rms_scale(x, x.shape[-1])) * g_ref[...]).astype(BF16)
    assert sum(FFN_CHUNK_TILES) * MXU_DIM == d_ff
    y = None
    lo = 0
    for tiles in FFN_CHUNK_TILES:
        tf = tiles * MXU_DIM
        gate = jnp.dot(h, win_ref[:, lo:lo + tf], preferred_element_type=F32)
        up = jnp.dot(h, win_ref[:, d_ff + lo:d_ff + lo + tf], preferred_element_type=F32)
        a = (gate * jax.nn.sigmoid(gate) * up).astype(BF16)
        part = jnp.dot(a, wout_ref[lo:lo + tf, :], preferred_element_type=F32)
        y = part if y is None else y + part
        lo += tf
    o_ref[...] = x + 0.5 * y


def _ffn(x, gain, w_in, w_out, layer):
    t, d = x.shape
    d_ff = w_out.shape[1]
    resident = pl.Buffered(1)
    return pl.pallas_call(
        functools.partial(_ffn_kernel, d_ff=d_ff),
        grid=(t // FFN_ROWS,),
        in_specs=[
            pl.BlockSpec((FFN_ROWS, d), lambda i: (i, 0)),
            pl.BlockSpec((None, 1, d), lambda i: (layer, 0, 0)),
            pl.BlockSpec((None, d, 2 * d_ff), lambda i: (layer, 0, 0), pipeline_mode=resident),
            pl.BlockSpec((None, d_ff, d), lambda i: (layer, 0, 0), pipeline_mode=resident),
        ],
        out_specs=pl.BlockSpec((FFN_ROWS, d), lambda i: (i, 0)),
        out_shape=jax.ShapeDtypeStruct((t, d), F32),
        compiler_params=pltpu.CompilerParams(
            dimension_semantics=("parallel",), vmem_limit_bytes=VMEM_LIMIT_BYTES),
        name="ffn",
    )(x, gain, w_in, w_out)


def _segment_sumsq(x, ones_ref):
    x2 = x * x
    hi = x2.astype(BF16)
    lo = (x2 - hi.astype(F32)).astype(BF16)
    w = x.shape[-1]
    step = min(w, MXU_DIM)
    ones = ones_ref[0:step, 0:step]
    parts = []
    for c in range(0, w, step):
        parts.append(jnp.dot(hi[:, c:c + step], ones, preferred_element_type=F32)
                     + jnp.dot(lo[:, c:c + step], ones, preferred_element_type=F32))
    return parts[0] if len(parts) == 1 else jnp.concatenate(parts, axis=-1)


def _lane_half_variants(x):
    swapped = pltpu.roll(x, HEAD_DIM, axis=1)
    low = lax.broadcasted_iota(jnp.int32, x.shape, 1) < HEAD_DIM
    zero = jnp.zeros_like(x)
    return jnp.concatenate([
        jnp.where(low, x, zero),
        jnp.where(low, zero, swapped),
        jnp.where(low, swapped, zero),
        jnp.where(low, zero, x),
    ], axis=-1)


def _mix_in_kernel(x_ref, g_ref, win_ref, sgu_g_ref, sgu_w_ref, sgu_b_ref, qg_ref, kg_ref,
                   ones_ref, outa_ref, q_ref, k4_ref, v4_ref):
    x = x_ref[...]
    rows = x.shape[0]
    h = ((x * _rms_scale(x, x.shape[-1])) * g_ref[...]).astype(BF16)
    p = jnp.dot(h, win_ref[...], preferred_element_type=F32)

    z = jax.nn.gelu(p[:, :2 * MIX_A])
    for g in range(SGU_GROUPS):
        u = z[:, g * LANES:(g + 1) * LANES]
        v = z[:, MIX_A + g * LANES:MIX_A + (g + 1) * LANES]
        vn = ((v * _rms_scale(v, LANES)) * sgu_g_ref[:, g * LANES:(g + 1) * LANES]).astype(BF16)
        w_s = sgu_w_ref[g]
        b_s = sgu_b_ref[g]
        for c in range(rows // CHUNK):
            r = slice(c * CHUNK, (c + 1) * CHUNK)
            s = jnp.dot(w_s, vn[r], preferred_element_type=F32) + b_s
            outa_ref[r, g * LANES:(g + 1) * LANES] = (u[r] * s).astype(BF16)

    q = p[:, 2 * MIX_A:2 * MIX_A + MIX_B]
    q_inv = lax.rsqrt(_segment_sumsq(q, ones_ref) * (1.0 / HEAD_DIM) + EPS)
    q_ref[...] = (((q * q_inv) * qg_ref[...]) * (HEAD_DIM ** -0.5)).astype(BF16)

    k = p[:, 2 * MIX_A + MIX_B:2 * MIX_A + MIX_B + KV_W]
    k_inv = lax.rsqrt(_segment_sumsq(k, ones_ref) * (1.0 / HEAD_DIM) + EPS)
    k4_ref[...] = _lane_half_variants((k * k_inv) * kg_ref[...]).astype(BF16)

    v_attn = p[:, 2 * MIX_A + MIX_B + KV_W:]
    v4_ref[...] = _lane_half_variants(v_attn).astype(BF16)


def _mix_in(x, gain, w_in, sgu_g, sgu_w, sgu_b, q_g, k_g, ones_bd, layer):
    t, d = x.shape
    in_cols = w_in.shape[-1]
    rows = MIX_IN_ROWS
    row_spec = lambda w: pl.BlockSpec((rows, w), lambda i: (i, 0))
    layer_spec = lambda *shape: pl.BlockSpec((None,) + shape, lambda i: (layer,) + (0,) * len(shape))
    return pl.pallas_call(
        _mix_in_kernel,
        grid=(t // rows,),
        in_specs=[
            row_spec(d),
            layer_spec(1, d),
            layer_spec(d, in_cols),
            layer_spec(1, MIX_A),
            layer_spec(SGU_GROUPS, CHUNK, CHUNK),
            layer_spec(SGU_GROUPS, CHUNK, LANES),
            layer_spec(1, MIX_B),
            layer_spec(1, KV_W),
            pl.BlockSpec((MXU_DIM, MXU_DIM), lambda i: (0, 0)),
        ],
        out_specs=[row_spec(MIX_A), row_spec(MIX_B), row_spec(N_VARIANTS * LANES),
                   row_spec(N_VARIANTS * LANES)],
        out_shape=[
            jax.ShapeDtypeStruct((t, MIX_A), BF16),
            jax.ShapeDtypeStruct((t, MIX_B), BF16),
            jax.ShapeDtypeStruct((t, N_VARIANTS * LANES), BF16),
            jax.ShapeDtypeStruct((t, N_VARIANTS * LANES), BF16),
        ],
        compiler_params=pltpu.CompilerParams(
            dimension_semantics=("parallel",), vmem_limit_bytes=VMEM_LIMIT_BYTES),
        name="mix_in",
    )(x, gain, w_in, sgu_g, sgu_w, sgu_b, q_g, k_g, ones_bd)


HALF_W = 3 * CHUNK
KV_COLS = 2 * LANES


def _mix_out_kernel(sink_ref, x_ref, outa_ref, q_ref, k4_ref, v4_ref, bias_ref, wout_ref,
                    o_ref, kpad_ref, vpad_ref, *, layer, n_blocks):
    j = pl.program_id(1)
    seq = k4_ref.shape[0]

    @pl.when(j == 0)
    def _():
        low = lax.broadcasted_iota(jnp.int32, (CHUNK, LANES), 1) < HEAD_DIM
        ones_low = jnp.where(low, 1.0, 0.0).astype(BF16)
        ones_high = jnp.where(low, 0.0, 1.0).astype(BF16)
        kpad_ref[0:CHUNK, :] = jnp.zeros((CHUNK, N_VARIANTS * LANES), BF16)
        kpad_ref[CHUNK:CHUNK + seq, :] = k4_ref[...]
        kpad_ref[CHUNK + seq:, :] = jnp.zeros((CHUNK, N_VARIANTS * LANES), BF16)
        for vi in range(N_VARIANTS):
            c0 = 2 * vi * LANES
            vpad_ref[0:CHUNK, c0:c0 + LANES] = jnp.zeros((CHUNK, LANES), BF16)
            vpad_ref[CHUNK:CHUNK + seq, c0:c0 + LANES] = v4_ref[:, vi * LANES:(vi + 1) * LANES]
            vpad_ref[CHUNK + seq:, c0:c0 + LANES] = jnp.zeros((CHUNK, LANES), BF16)
            ones = ones_low if vi % 2 == 0 else ones_high
            for r0 in range(0, seq + 2 * CHUNK, CHUNK):
                vpad_ref[r0:r0 + CHUNK, c0 + LANES:c0 + 2 * LANES] = ones

    blocks_per_step = x_ref.shape[0] // CHUNK
    first_rows = lax.broadcasted_iota(jnp.int32, (2 * CHUNK, 1), 0) < CHUNK
    low_lanes = lax.broadcasted_iota(jnp.int32, (2 * CHUNK, LANES), 1) < HEAD_DIM
    sink_cols = [[jnp.where(first_rows, sink_ref[layer, 4 * kvh + half], sink_ref[layer, 4 * kvh + half + 2])
                  for half in range(2)] for kvh in range(N_KV_HEADS)]

    def scores(i, kvh):
        n = j * blocks_per_step + i
        edge = jnp.where(n == 0, 0, jnp.where(n == n_blocks - 1, 2, 1))
        win = pl.ds(pl.multiple_of(n * CHUNK, CHUNK), HALF_W)
        r = slice(i * CHUNK, (i + 1) * CHUNK)
        cg = 2 * kvh
        lhs = jnp.concatenate([q_ref[r, cg * LANES:(cg + 1) * LANES],
                               q_ref[r, (cg + 1) * LANES:(cg + 2) * LANES]], axis=0)
        keys = jnp.concatenate([kpad_ref[win, kvh * KV_COLS:kvh * KV_COLS + LANES],
                                kpad_ref[win, kvh * KV_COLS + LANES:(kvh + 1) * KV_COLS]], axis=0)
        s = lax.dot_general(lhs, keys, (((1,), (1,)), ((), ())), preferred_element_type=F32)
        s = s + bias_ref[edge, kvh]
        m = [jnp.maximum(jnp.max(s[:, h * HALF_W:(h + 1) * HALF_W], axis=-1, keepdims=True),
                         sink_cols[kvh][h]) for h in range(2)]
        return s, m, win

    def outputs(kvh, s, m, win):
        e = jnp.concatenate([jnp.exp(s[:, h * HALF_W:(h + 1) * HALF_W] - m[h]) for h in range(2)],
                            axis=-1).astype(BF16)
        vals = jnp.concatenate([vpad_ref[win, 2 * kvh * KV_COLS:(2 * kvh + 1) * KV_COLS],
                                vpad_ref[win, (2 * kvh + 1) * KV_COLS:(2 * kvh + 2) * KV_COLS]], axis=0)
        pv = jnp.dot(e, vals, preferred_element_type=F32)
        e_sink = jnp.where(low_lanes, jnp.exp(sink_cols[kvh][0] - m[0]),
                           jnp.exp(sink_cols[kvh][1] - m[1]))
        o_kv = pv[:, :LANES] * (1.0 / (pv[:, LANES:] + e_sink))
        return [o_kv[0:CHUNK].astype(BF16), o_kv[CHUNK:].astype(BF16)]

    work = [(i, kvh) for i in range(blocks_per_step) for kvh in range(N_KV_HEADS)]
    tiles = []
    pending = scores(*work[0])
    for idx, (i, kvh) in enumerate(work):
        nxt = scores(*work[idx + 1]) if idx + 1 < len(work) else None
        tiles += outputs(kvh, *pending)
        pending = nxt
    per_block = 2 * N_KV_HEADS
    attn = jnp.concatenate([jnp.concatenate(tiles[b * per_block:(b + 1) * per_block], axis=-1)
                            for b in range(blocks_per_step)], axis=0)

    y = (jnp.dot(outa_ref[...], wout_ref[0:MIX_A, :], preferred_element_type=F32)
         + jnp.dot(attn, wout_ref[MIX_A:, :], preferred_element_type=F32))
    o_ref[...] = x_ref[...] + y


def _mix_out(x, outa, q, k4, v4, sink, bias_tab, w_out, layer, batch):
    t, d = x.shape
    seq = t // batch
    rows = MIX_OUT_ROWS
    steps = seq // rows
    resident = pl.Buffered(1)
    row_spec = lambda w: pl.BlockSpec((rows, w), lambda b, j: (b * steps + j, 0))
    seq_spec = pl.BlockSpec((seq, N_VARIANTS * LANES), lambda b, j: (b, 0))
    return pl.pallas_call(
        functools.partial(_mix_out_kernel, layer=layer, n_blocks=seq // CHUNK),
        grid=(batch, steps),
        in_specs=[
            pl.BlockSpec(memory_space=pltpu.SMEM),
            row_spec(d), row_spec(MIX_A), row_spec(MIX_B), seq_spec, seq_spec,
            pl.BlockSpec(bias_tab.shape, lambda b, j: (0, 0, 0, 0), pipeline_mode=resident),
            pl.BlockSpec((None, MIX_A + MIX_B, d), lambda b, j: (layer, 0, 0), pipeline_mode=resident),
        ],
        out_specs=row_spec(d),
        out_shape=jax.ShapeDtypeStruct((t, d), F32),
        scratch_shapes=[
            pltpu.VMEM((seq + 2 * CHUNK, N_VARIANTS * LANES), BF16),
            pltpu.VMEM((seq + 2 * CHUNK, 2 * N_VARIANTS * LANES), BF16),
        ],
        compiler_params=pltpu.CompilerParams(
            dimension_semantics=("parallel", "arbitrary"), vmem_limit_bytes=VMEM_LIMIT_BYTES),
        name="mix_out",
    )(sink, x, outa, q, k4, v4, bias_tab, w_out)


def _t5_bucket(rel):
    nb = N_BUCKETS // 2
    ret = (rel > 0).astype(np.int32) * nb
    n = np.abs(rel)
    max_exact = nb // 2
    large = max_exact + (np.log(np.maximum(n, 1).astype(np.float32) / max_exact)
                         / math.log(MAX_DISTANCE / max_exact) * (nb - max_exact)).astype(np.int32)
    large = np.minimum(large, nb - 1)
    return ret + np.where(n < max_exact, n, large).astype(np.int32)


def _bias_table(rel_bias):
    rel = np.arange(3 * CHUNK)[None, :] - CHUNK - np.arange(CHUNK)[:, None]
    bucket = _t5_bucket(rel)
    rb = rel_bias.astype(F32)
    bias = jnp.zeros((N_Q_HEADS,) + rel.shape, F32)
    for b in range(N_BUCKETS):
        bias = jnp.where((bucket == b)[None], rb[b][:, None, None], bias)
    band = np.abs(rel) <= CHUNK
    col = np.arange(3 * CHUNK)[None, :]
    edge_masks = [band & (col >= CHUNK), band, band & (col < 2 * CHUNK)]
    head_rows = np.array([[4 * (vi // 2) + (vi % 2), 4 * (vi // 2) + (vi % 2) + 2]
                          for vi in range(N_VARIANTS)])
    per_variant = jnp.stack([jnp.concatenate([bias[a], bias[b]], axis=0) for a, b in head_rows])
    tabs = [jnp.where(np.tile(mask, (2, 1))[None], per_variant, NEG_INF) for mask in edge_masks]
    tab = jnp.stack(tabs, axis=0)
    return jnp.concatenate([tab[:, 0::2], tab[:, 1::2]], axis=-1)


def kernel(x, ffn1_norm, ffn1_w_in, ffn1_w_out, mix_norm, w_mix_in, sgu_norm, sgu_w, sgu_b,
           q_norm, k_norm, sink, rel_bias, w_mix_out, ffn2_norm, ffn2_w_in, ffn2_w_out):
    batch, seq, d = x.shape
    depth = ffn1_norm.shape[0]
    assert seq % MIX_OUT_ROWS == 0 and (batch * seq) % FFN_ROWS == 0

    row3 = lambda a: a.astype(F32)[:, None, :]
    ffn1_g, mix_g, ffn2_g, sgu_g = row3(ffn1_norm), row3(mix_norm), row3(ffn2_norm), row3(sgu_norm)
    q_g = row3(jnp.tile(q_norm, (1, N_Q_HEADS)))
    k_g = row3(jnp.tile(k_norm, (1, N_KV_HEADS)))
    sgu_b3 = jnp.broadcast_to(sgu_b.astype(F32)[..., None], sgu_b.shape + (LANES,))
    ones_bd = jnp.asarray(np.kron(np.eye(MXU_DIM // HEAD_DIM), np.ones((HEAD_DIM, HEAD_DIM))), BF16)
    bias_tab = _bias_table(rel_bias)
    sink = sink.astype(F32)
    bf = lambda w: w.astype(BF16)
    ffn1_wi, ffn1_wo, ffn2_wi, ffn2_wo = bf(ffn1_w_in), bf(ffn1_w_out), bf(ffn2_w_in), bf(ffn2_w_out)
    mix_wi, mix_wo, sgu_wb = bf(w_mix_in), bf(w_mix_out), bf(sgu_w)

    xt = x.reshape(batch * seq, d)
    for l in range(depth):
        xt = _ffn(xt, ffn1_g, ffn1_wi, ffn1_wo, l)
        outa, q, k4, v4 = _mix_in(xt, mix_g, mix_wi, sgu_g, sgu_wb, sgu_b3, q_g, k_g, ones_bd, l)
        xt = _mix_out(xt, outa, q, k4, v4, sink, bias_tab, mix_wo, l, batch)
        xt = _ffn(xt, ffn2_g, ffn2_wi, ffn2_wo, l)
    return xt.reshape(batch, seq, d)
```

```python
import functools
import math

import jax
import jax.numpy as jnp
import numpy as np
from jax import lax
from jax.experimental import pallas as pl
from jax.experimental.pallas import tpu as pltpu

F32 = jnp.float32
BF16 = jnp.bfloat16

EPS = 1e-6
NEG_INF = -1e30

LANES = 128
MXU_DIM = 256
VMEM_LIMIT_BYTES = 56 * 1024 * 1024

SGU_GROUPS = 4
CHUNK = 128
N_Q_HEADS = 8
N_KV_HEADS = 2
HEAD_DIM = 64
N_BUCKETS = 32
MAX_DISTANCE = 128
MIX_A = SGU_GROUPS * LANES
MIX_B = N_Q_HEADS * HEAD_DIM
KV_W = N_KV_HEADS * HEAD_DIM
N_VARIANTS = 4

FFN_ROWS = 1024
FFN_CHUNK_TILES = (6, 5)
MIX_IN_ROWS = 512
MIX_OUT_ROWS = 512


def _rms_scale(x, width):
    return lax.rsqrt(jnp.sum(x * x, axis=-1, keepdims=True) * (1.0 / width) + EPS)


def _ffn_kernel(x_ref, g_ref, win_ref, wout_ref, o_ref, *, d_ff):
    x = x_ref[...]
    h = ((x * _rms_scale(x, x.shape[-1])) * g_ref[...]).astype(BF16)
    assert sum(FFN_CHUNK_TILES) * MXU_DIM == d_ff
    y = None
    lo = 0
    for tiles in FFN_CHUNK_TILES:
        tf = tiles * MXU_DIM
        gate = jnp.dot(h, win_ref[:, lo:lo + tf], preferred_element_type=F32)
        up = jnp.dot(h, win_ref[:, d_ff + lo:d_ff + lo + tf], preferred_element_type=F32)
        a = (gate * jax.nn.sigmoid(gate) * up).astype(BF16)
        part = jnp.dot(a, wout_ref[lo:lo + tf, :], preferred_element_type=F32)
        y = part if y is None else y + part
        lo += tf
    o_ref[...] = x + 0.5 * y


def _ffn(x, gain, w_in, w_out, layer):
    t, d = x.shape
    d_ff = w_out.shape[1]
    resident = pl.Buffered(1)
    return pl.pallas_call(
        functools.partial(_ffn_kernel, d_ff=d_ff),
        grid=(t // FFN_ROWS,),
        in_specs=[
            pl.BlockSpec((FFN_ROWS, d), lambda i: (i, 0)),
            pl.BlockSpec((None, 1, d), lambda i: (layer, 0, 0)),
            pl.BlockSpec((None, d, 2 * d_ff), lambda i: (layer, 0, 0), pipeline_mode=resident),
            pl.BlockSpec((None, d_ff, d), lambda i: (layer, 0, 0), pipeline_mode=resident),
        ],
        out_specs=pl.BlockSpec((FFN_ROWS, d), lambda i: (i, 0)),
        out_shape=jax.ShapeDtypeStruct((t, d), F32),
        compiler_params=pltpu.CompilerParams(
            dimension_semantics=("parallel",), vmem_limit_bytes=VMEM_LIMIT_BYTES),
        name="ffn",
    )(x, gain, w_in, w_out)


def _segment_sumsq(x, ones_ref):
    x2 = x * x
    hi = x2.astype(BF16)
    lo = (x2 - hi.astype(F32)).astype(BF16)
    w = x.shape[-1]
    step = min(w, MXU_DIM)
    ones = ones_ref[0:step, 0:step]
    parts = []
    for c in range(0, w, step):
        parts.append(jnp.dot(hi[:, c:c + step], ones, preferred_element_type=F32)
                     + jnp.dot(lo[:, c:c + step], ones, preferred_element_type=F32))
    return parts[0] if len(parts) == 1 else jnp.concatenate(parts, axis=-1)


def _lane_half_variants(x):
    swapped = pltpu.roll(x, HEAD_DIM, axis=1)
    low = lax.broadcasted_iota(jnp.int32, x.shape, 1) < HEAD_DIM
    zero = jnp.zeros_like(x)
    return [
        jnp.where(low, x, zero),
        jnp.where(low, zero, swapped),
        jnp.where(low, swapped, zero),
        jnp.where(low, zero, x),
    ]


def _mix_in_kernel(x_ref, g_ref, win_ref, sgu_g_ref, sgu_w_ref, sgu_b_ref, qg_ref, kg_ref,
                   ones_ref, outa_ref, q_ref, k_ref, v_ref):
    x = x_ref[...]
    rows = x.shape[0]
    h = ((x * _rms_scale(x, x.shape[-1])) * g_ref[...]).astype(BF16)
    p = jnp.dot(h, win_ref[...], preferred_element_type=F32)

    z = jax.nn.gelu(p[:, :2 * MIX_A])
    for g in range(SGU_GROUPS):
        u = z[:, g * LANES:(g + 1) * LANES]
        v = z[:, MIX_A + g * LANES:MIX_A + (g + 1) * LANES]
        vn = ((v * _rms_scale(v, LANES)) * sgu_g_ref[:, g * LANES:(g + 1) * LANES]).astype(BF16)
        w_s = sgu_w_ref[g]
        b_s = sgu_b_ref[g]
        for c in range(rows // CHUNK):
            r = slice(c * CHUNK, (c + 1) * CHUNK)
            s = jnp.dot(w_s, vn[r], preferred_element_type=F32) + b_s
            outa_ref[r, g * LANES:(g + 1) * LANES] = (u[r] * s).astype(BF16)

    q = p[:, 2 * MIX_A:2 * MIX_A + MIX_B]
    q_inv = lax.rsqrt(_segment_sumsq(q, ones_ref) * (1.0 / HEAD_DIM) + EPS)
    q_ref[...] = (((q * q_inv) * qg_ref[...]) * (HEAD_DIM ** -0.5)).astype(BF16)

    k = p[:, 2 * MIX_A + MIX_B:2 * MIX_A + MIX_B + KV_W]
    k_inv = lax.rsqrt(_segment_sumsq(k, ones_ref) * (1.0 / HEAD_DIM) + EPS)
    k_ref[...] = ((k * k_inv) * kg_ref[...]).astype(BF16)
    v_ref[...] = p[:, 2 * MIX_A + MIX_B + KV_W:].astype(BF16)


def _mix_in(x, gain, w_in, sgu_g, sgu_w, sgu_b, q_g, k_g, ones_bd, layer):
    t, d = x.shape
    in_cols = w_in.shape[-1]
    rows = MIX_IN_ROWS
    row_spec = lambda w: pl.BlockSpec((rows, w), lambda i: (i, 0))
    layer_spec = lambda *shape: pl.BlockSpec((None,) + shape, lambda i: (layer,) + (0,) * len(shape))
    return pl.pallas_call(
        _mix_in_kernel,
        grid=(t // rows,),
        in_specs=[
            row_spec(d),
            layer_spec(1, d),
            layer_spec(d, in_cols),
            layer_spec(1, MIX_A),
            layer_spec(SGU_GROUPS, CHUNK, CHUNK),
            layer_spec(SGU_GROUPS, CHUNK, LANES),
            layer_spec(1, MIX_B),
            layer_spec(1, KV_W),
            pl.BlockSpec((MXU_DIM, MXU_DIM), lambda i: (0, 0)),
        ],
        out_specs=[row_spec(MIX_A), row_spec(MIX_B), row_spec(KV_W), row_spec(KV_W)],
        out_shape=[
            jax.ShapeDtypeStruct((t, MIX_A), BF16),
            jax.ShapeDtypeStruct((t, MIX_B), BF16),
            jax.ShapeDtypeStruct((t, KV_W), BF16),
            jax.ShapeDtypeStruct((t, KV_W), BF16),
        ],
        compiler_params=pltpu.CompilerParams(
            dimension_semantics=("parallel",), vmem_limit_bytes=VMEM_LIMIT_BYTES),
        name="mix_in",
    )(x, gain, w_in, sgu_g, sgu_w, sgu_b, q_g, k_g, ones_bd)


HALF_W = 3 * CHUNK
KV_COLS = 2 * LANES


def _mix_out_kernel(sink_ref, x_ref, outa_ref, q_ref, k_ref, v_ref, bias_ref, wout_ref,
                    o_ref, kpad_ref, vpad_ref, *, layer, n_blocks):
    j = pl.program_id(1)
    seq = k_ref.shape[0]

    @pl.when(j == 0)
    def _():
        low = lax.broadcasted_iota(jnp.int32, (CHUNK, LANES), 1) < HEAD_DIM
        ones_low = jnp.where(low, 1.0, 0.0).astype(BF16)
        ones_high = jnp.where(low, 0.0, 1.0).astype(BF16)
        kpad_ref[0:CHUNK, :] = jnp.zeros((CHUNK, N_VARIANTS * LANES), BF16)
        kpad_ref[CHUNK + seq:, :] = jnp.zeros((CHUNK, N_VARIANTS * LANES), BF16)
        k_variants = _lane_half_variants(k_ref[...].astype(F32))
        v_variants = _lane_half_variants(v_ref[...].astype(F32))
        for vi in range(N_VARIANTS):
            kpad_ref[CHUNK:CHUNK + seq, vi * LANES:(vi + 1) * LANES] = k_variants[vi].astype(BF16)
            c0 = 2 * vi * LANES
            vpad_ref[0:CHUNK, c0:c0 + LANES] = jnp.zeros((CHUNK, LANES), BF16)
            vpad_ref[CHUNK:CHUNK + seq, c0:c0 + LANES] = v_variants[vi].astype(BF16)
            vpad_ref[CHUNK + seq:, c0:c0 + LANES] = jnp.zeros((CHUNK, LANES), BF16)
            ones = ones_low if vi % 2 == 0 else ones_high
            for r0 in range(0, seq + 2 * CHUNK, CHUNK):
                vpad_ref[r0:r0 + CHUNK, c0 + LANES:c0 + 2 * LANES] = ones

    blocks_per_step = x_ref.shape[0] // CHUNK
    first_rows = lax.broadcasted_iota(jnp.int32, (2 * CHUNK, 1), 0) < CHUNK
    low_lanes = lax.broadcasted_iota(jnp.int32, (2 * CHUNK, LANES), 1) < HEAD_DIM
    sink_cols = [[jnp.where(first_rows, sink_ref[layer, 4 * kvh + half], sink_ref[layer, 4 * kvh + half + 2])
                  for half in range(2)] for kvh in range(N_KV_HEADS)]

    def scores(i, kvh):
        n = j * blocks_per_step + i
        edge = jnp.where(n == 0, 0, jnp.where(n == n_blocks - 1, 2, 1))
        win = pl.ds(pl.multiple_of(n * CHUNK, CHUNK), HALF_W)
        r = slice(i * CHUNK, (i + 1) * CHUNK)
        cg = 2 * kvh
        lhs = jnp.concatenate([q_ref[r, cg * LANES:(cg + 1) * LANES],
                               q_ref[r, (cg + 1) * LANES:(cg + 2) * LANES]], axis=0)
        keys = jnp.concatenate([kpad_ref[win, kvh * KV_COLS:kvh * KV_COLS + LANES],
                                kpad_ref[win, kvh * KV_COLS + LANES:(kvh + 1) * KV_COLS]], axis=0)
        s = lax.dot_general(lhs, keys, (((1,), (1,)), ((), ())), preferred_element_type=F32)
        s = s + bias_ref[edge, kvh]
        m = [jnp.maximum(jnp.max(s[:, h * HALF_W:(h + 1) * HALF_W], axis=-1, keepdims=True),
                         sink_cols[kvh][h]) for h in range(2)]
        return s, m, win

    def outputs(kvh, s, m, win):
        e = jnp.concatenate([jnp.exp(s[:, h * HALF_W:(h + 1) * HALF_W] - m[h]) for h in range(2)],
                            axis=-1).astype(BF16)
        vals = jnp.concatenate([vpad_ref[win, 2 * kvh * KV_COLS:(2 * kvh + 1) * KV_COLS],
                                vpad_ref[win, (2 * kvh + 1) * KV_COLS:(2 * kvh + 2) * KV_COLS]], axis=0)
        pv = jnp.dot(e, vals, preferred_element_type=F32)
        e_sink = jnp.where(low_lanes, jnp.exp(sink_cols[kvh][0] - m[0]),
                           jnp.exp(sink_cols[kvh][1] - m[1]))
        o_kv = pv[:, :LANES] * (1.0 / (pv[:, LANES:] + e_sink))
        return [o_kv[0:CHUNK].astype(BF16), o_kv[CHUNK:].astype(BF16)]

    work = [(i, kvh) for i in range(blocks_per_step) for kvh in range(N_KV_HEADS)]
    tiles = []
    pending = scores(*work[0])
    for idx, (i, kvh) in enumerate(work):
        nxt = scores(*work[idx + 1]) if idx + 1 < len(work) else None
        tiles += outputs(kvh, *pending)
        pending = nxt
    per_block = 2 * N_KV_HEADS
    attn = jnp.concatenate([jnp.concatenate(tiles[b * per_block:(b + 1) * per_block], axis=-1)
                            for b in range(blocks_per_step)], axis=0)

    y = (jnp.dot(outa_ref[...], wout_ref[0:MIX_A, :], preferred_element_type=F32)
         + jnp.dot(attn, wout_ref[MIX_A:, :], preferred_element_type=F32))
    o_ref[...] = x_ref[...] + y


def _mix_out(x, outa, q, k, v, sink, bias_tab, w_out, layer, batch):
    t, d = x.shape
    seq = t // batch
    rows = MIX_OUT_ROWS
    steps = seq // rows
    resident = pl.Buffered(1)
    row_spec = lambda w: pl.BlockSpec((rows, w), lambda b, j: (b * steps + j, 0))
    seq_spec = pl.BlockSpec((seq, KV_W), lambda b, j: (b, 0))
    return pl.pallas_call(
        functools.partial(_mix_out_kernel, layer=layer, n_blocks=seq // CHUNK),
        grid=(batch, steps),
        in_specs=[
            pl.BlockSpec(memory_space=pltpu.SMEM),
            row_spec(d), row_spec(MIX_A), row_spec(MIX_B), seq_spec, seq_spec,
            pl.BlockSpec(bias_tab.shape, lambda b, j: (0, 0, 0, 0), pipeline_mode=resident),
            pl.BlockSpec((None, MIX_A + MIX_B, d), lambda b, j: (layer, 0, 0), pipeline_mode=resident),
        ],
        out_specs=row_spec(d),
        out_shape=jax.ShapeDtypeStruct((t, d), F32),
        scratch_shapes=[
            pltpu.VMEM((seq + 2 * CHUNK, N_VARIANTS * LANES), BF16),
            pltpu.VMEM((seq + 2 * CHUNK, 2 * N_VARIANTS * LANES), BF16),
        ],
        compiler_params=pltpu.CompilerParams(
            dimension_semantics=("parallel", "arbitrary"), vmem_limit_bytes=VMEM_LIMIT_BYTES),
        name="mix_out",
    )(sink, x, outa, q, k, v, bias_tab, w_out)


def _t5_bucket(rel):
    nb = N_BUCKETS // 2
    ret = (rel > 0).astype(np.int32) * nb
    n = np.abs(rel)
    max_exact = nb // 2
    large = max_exact + (np.log(np.maximum(n, 1).astype(np.float32) / max_exact)
                         / math.log(MAX_DISTANCE / max_exact) * (nb - max_exact)).astype(np.int32)
    large = np.minimum(large, nb - 1)
    return ret + np.where(n < max_exact, n, large).astype(np.int32)


def _bias_table(rel_bias):
    rel = np.arange(3 * CHUNK)[None, :] - CHUNK - np.arange(CHUNK)[:, None]
    bucket = _t5_bucket(rel)
    rb = rel_bias.astype(F32)
    bias = jnp.zeros((N_Q_HEADS,) + rel.shape, F32)
    for b in range(N_BUCKETS):
        bias = jnp.where((bucket == b)[None], rb[b][:, None, None], bias)
    band = np.abs(rel) <= CHUNK
    col = np.arange(3 * CHUNK)[None, :]
    edge_masks = [band & (col >= CHUNK), band, band & (col < 2 * CHUNK)]
    head_rows = np.array([[4 * (vi // 2) + (vi % 2), 4 * (vi // 2) + (vi % 2) + 2]
                          for vi in range(N_VARIANTS)])
    per_variant = jnp.stack([jnp.concatenate([bias[a], bias[b]], axis=0) for a, b in head_rows])
    tabs = [jnp.where(np.tile(mask, (2, 1))[None], per_variant, NEG_INF) for mask in edge_masks]
    tab = jnp.stack(tabs, axis=0)
    return jnp.concatenate([tab[:, 0::2], tab[:, 1::2]], axis=-1)


def kernel(x, ffn1_norm, ffn1_w_in, ffn1_w_out, mix_norm, w_mix_in, sgu_norm, sgu_w, sgu_b,
           q_norm, k_norm, sink, rel_bias, w_mix_out, ffn2_norm, ffn2_w_in, ffn2_w_out):
    batch, seq, d = x.shape
    depth = ffn1_norm.shape[0]
    assert seq % MIX_OUT_ROWS == 0 and (batch * seq) % FFN_ROWS == 0

    row3 = lambda a: a.astype(F32)[:, None, :]
    ffn1_g, mix_g, ffn2_g, sgu_g = row3(ffn1_norm), row3(mix_norm), row3(ffn2_norm), row3(sgu_norm)
    q_g = row3(jnp.tile(q_norm, (1, N_Q_HEADS)))
    k_g = row3(jnp.tile(k_norm, (1, N_KV_HEADS)))
    sgu_b3 = jnp.broadcast_to(sgu_b.astype(F32)[..., None], sgu_b.shape + (LANES,))
    ones_bd = jnp.asarray(np.kron(np.eye(MXU_DIM // HEAD_DIM), np.ones((HEAD_DIM, HEAD_DIM))), BF16)
    bias_tab = _bias_table(rel_bias)
    sink = sink.astype(F32)
    bf = lambda w: w.astype(BF16)
    ffn1_wi, ffn1_wo, ffn2_wi, ffn2_wo = bf(ffn1_w_in), bf(ffn1_w_out), bf(ffn2_w_in), bf(ffn2_w_out)
    mix_wi, mix_wo, sgu_wb = bf(w_mix_in), bf(w_mix_out), bf(sgu_w)

    xt = x.reshape(batch * seq, d)
    for l in range(depth):
        xt = _ffn(xt, ffn1_g, ffn1_wi, ffn1_wo, l)
        outa, q, k, v = _mix_in(xt, mix_g, mix_wi, sgu_g, sgu_wb, sgu_b3, q_g, k_g, ones_bd, l)
        xt = _mix_out(xt, outa, q, k, v, sink, bias_tab, mix_wo, l, batch)
        xt = _ffn(xt, ffn2_g, ffn2_wi, ffn2_wo, l)
    return xt.reshape(batch, seq, d)
```

```python
import functools
import math

import jax
import jax.numpy as jnp
import numpy as np
from jax import lax
from jax.experimental import pallas as pl
from jax.experimental.pallas import tpu as pltpu

F32 = jnp.float32
BF16 = jnp.bfloat16

EPS = 1e-6
NEG_INF = -1e30

LANES = 128
MXU_DIM = 256
VMEM_LIMIT_BYTES = 56 * 1024 * 1024

SGU_GROUPS = 4
CHUNK = 128
N_Q_HEADS = 8
N_KV_HEADS = 2
HEAD_DIM = 64
N_BUCKETS = 32
MAX_DISTANCE = 128
MIX_A = SGU_GROUPS * LANES
MIX_B = N_Q_HEADS * HEAD_DIM
KV_W = N_KV_HEADS * HEAD_DIM
N_VARIANTS = 4

FFN_ROWS = 1024
FFN_CHUNK_TILES = (6, 5)
MIX_IN_ROWS = 1024
MIX_OUT_ROWS = 1024


def _rms_scale(x, width):
    return lax.rsqrt(jnp.sum(x * x, axis=-1, keepdims=True) * (1.0 / width) + EPS)


def _ffn_kernel(*refs, d_ff, n_casts):
    x_ref, g_ref, win_ref, wout_ref = refs[:4]
    src_refs = refs[4:4 + n_casts]
    o_ref = refs[4 + n_casts]
    dst_refs = refs[5 + n_casts:]
    for src_ref, dst_ref in zip(src_refs, dst_refs):
        dst_ref[...] = src_ref[...].astype(BF16)
    x = x_ref[...]
    h = ((x * _rms_scale(x, x.shape[-1])) * g_ref[...]).astype(BF16)
    assert sum(FFN_CHUNK_TILES) * MXU_DIM == d_ff
    y = None
    lo = 0
    for tiles in FFN_CHUNK_TILES:
        tf = tiles * MXU_DIM
        gate = jnp.dot(h, win_ref[:, lo:lo + tf], preferred_element_type=F32)
        up = jnp.dot(h, win_ref[:, d_ff + lo:d_ff + lo + tf], preferred_element_type=F32)
        a = (gate * jax.nn.sigmoid(gate) * up).astype(BF16)
        part = jnp.dot(a, wout_ref[lo:lo + tf, :], preferred_element_type=F32)
        y = part if y is None else y + part
        lo += tf
    o_ref[...] = x + 0.5 * y


def _ffn(x, gain, w_in, w_out, layer, casts=()):
    t, d = x.shape
    d_ff = w_out.shape[0]
    steps = t // FFN_ROWS
    resident = pl.Buffered(1)
    cast_in_specs, cast_out_specs, cast_out_shapes = [], [], []
    for w, l in casts:
        _, rows, cols = w.shape
        slab = rows // steps
        assert slab * steps == rows and slab % 16 == 0
        cast_in_specs.append(pl.BlockSpec((None, slab, cols), lambda i, l=l: (l, i, 0)))
        cast_out_specs.append(pl.BlockSpec((slab, cols), lambda i: (i, 0)))
        cast_out_shapes.append(jax.ShapeDtypeStruct((rows, cols), BF16))
    outs = pl.pallas_call(
        functools.partial(_ffn_kernel, d_ff=d_ff, n_casts=len(casts)),
        grid=(steps,),
        in_specs=[
            pl.BlockSpec((FFN_ROWS, d), lambda i: (i, 0)),
            pl.BlockSpec((None, 1, d), lambda i: (layer, 0, 0)),
            pl.BlockSpec((d, 2 * d_ff), lambda i: (0, 0), pipeline_mode=resident),
            pl.BlockSpec((d_ff, d), lambda i: (0, 0), pipeline_mode=resident),
        ] + cast_in_specs,
        out_specs=[pl.BlockSpec((FFN_ROWS, d), lambda i: (i, 0))] + cast_out_specs,
        out_shape=[jax.ShapeDtypeStruct((t, d), F32)] + cast_out_shapes,
        compiler_params=pltpu.CompilerParams(
            dimension_semantics=("parallel",), vmem_limit_bytes=VMEM_LIMIT_BYTES),
        name="ffn",
    )(x, gain, w_in, w_out, *[w for w, _ in casts])
    return outs[0], outs[1:]


def _segment_sumsq(x, ones_ref):
    x2 = x * x
    hi = x2.astype(BF16)
    lo = (x2 - hi.astype(F32)).astype(BF16)
    w = x.shape[-1]
    step = min(w, MXU_DIM)
    ones = ones_ref[0:step, 0:step]
    ones2 = jnp.concatenate([ones, ones], axis=0)
    parts = []
    for c in range(0, w, step):
        parts.append(jnp.dot(jnp.concatenate([hi[:, c:c + step], lo[:, c:c + step]], axis=1),
                             ones2, preferred_element_type=F32))
    return parts[0] if len(parts) == 1 else jnp.concatenate(parts, axis=-1)


def _lane_half_variants(x):
    swapped = pltpu.roll(x, HEAD_DIM, axis=1)
    low = lax.broadcasted_iota(jnp.int32, x.shape, 1) < HEAD_DIM
    zero = jnp.zeros_like(x)
    return [
        jnp.where(low, x, zero),
        jnp.where(low, zero, swapped),
        jnp.where(low, swapped, zero),
        jnp.where(low, zero, x),
    ]


def _mix_in_kernel(x_ref, g_ref, win_ref, sgu_g_ref, sgu_w_ref, sgu_b_ref, qg_ref, kg_ref,
                   ones_ref, outa_ref, q_ref, k_ref, v_ref):
    x = x_ref[...]
    rows = x.shape[0]
    h = ((x * _rms_scale(x, x.shape[-1])) * g_ref[...]).astype(BF16)
    p = jnp.dot(h, win_ref[...], preferred_element_type=F32)

    z = jax.nn.gelu(p[:, :2 * MIX_A])
    for g in range(SGU_GROUPS):
        u = z[:, g * LANES:(g + 1) * LANES]
        v = z[:, MIX_A + g * LANES:MIX_A + (g + 1) * LANES]
        vn = ((v * _rms_scale(v, LANES)) * sgu_g_ref[:, g * LANES:(g + 1) * LANES]).astype(BF16)
        w_s = sgu_w_ref[g].astype(BF16)
        b_s = sgu_b_ref[g]
        chunks = [slice(c * CHUNK, (c + 1) * CHUNK) for c in range(rows // CHUNK)]
        s = jnp.dot(w_s, jnp.concatenate([vn[r] for r in chunks], axis=1),
                    preferred_element_type=F32)
        for c, r in enumerate(chunks):
            outa_ref[r, g * LANES:(g + 1) * LANES] = (
                u[r] * (s[:, c * LANES:(c + 1) * LANES] + b_s)).astype(BF16)

    q = p[:, 2 * MIX_A:2 * MIX_A + MIX_B]
    q_inv = lax.rsqrt(_segment_sumsq(q, ones_ref) * (1.0 / HEAD_DIM) + EPS)
    q_ref[...] = (((q * q_inv) * qg_ref[...]) * (HEAD_DIM ** -0.5)).astype(BF16)

    k = p[:, 2 * MIX_A + MIX_B:2 * MIX_A + MIX_B + KV_W]
    k_inv = lax.rsqrt(_segment_sumsq(k, ones_ref) * (1.0 / HEAD_DIM) + EPS)
    k_ref[...] = ((k * k_inv) * kg_ref[...]).astype(BF16)
    v_ref[...] = p[:, 2 * MIX_A + MIX_B + KV_W:].astype(BF16)


def _mix_in(x, gain, w_in, sgu_g, sgu_w, sgu_b, q_g, k_g, ones_bd, layer):
    t, d = x.shape
    in_cols = w_in.shape[-1]
    rows = MIX_IN_ROWS
    row_spec = lambda w: pl.BlockSpec((rows, w), lambda i: (i, 0))
    layer_spec = lambda *shape: pl.BlockSpec((None,) + shape, lambda i: (layer,) + (0,) * len(shape))
    return pl.pallas_call(
        _mix_in_kernel,
        grid=(t // rows,),
        in_specs=[
            row_spec(d),
            layer_spec(1, d),
            pl.BlockSpec((d, in_cols), lambda i: (0, 0)),
            layer_spec(1, MIX_A),
            layer_spec(SGU_GROUPS, CHUNK, CHUNK),
            layer_spec(SGU_GROUPS, CHUNK, LANES),
            layer_spec(1, MIX_B),
            layer_spec(1, KV_W),
            pl.BlockSpec((MXU_DIM, MXU_DIM), lambda i: (0, 0)),
        ],
        out_specs=[row_spec(MIX_A), row_spec(MIX_B), row_spec(KV_W), row_spec(KV_W)],
        out_shape=[
            jax.ShapeDtypeStruct((t, MIX_A), BF16),
            jax.ShapeDtypeStruct((t, MIX_B), BF16),
            jax.ShapeDtypeStruct((t, KV_W), BF16),
            jax.ShapeDtypeStruct((t, KV_W), BF16),
        ],
        compiler_params=pltpu.CompilerParams(
            dimension_semantics=("parallel",), vmem_limit_bytes=VMEM_LIMIT_BYTES),
        name="mix_in",
    )(x, gain, w_in, sgu_g, sgu_w, sgu_b, q_g, k_g, ones_bd)


HALF_W = 3 * CHUNK
KV_COLS = 2 * LANES


def _mix_out_kernel(sink_ref, x_ref, outa_ref, q_ref, k_ref, v_ref, bias_ref, wout_ref,
                    o_ref, kpad_ref, vpad_ref, *, layer, n_blocks):
    j = pl.program_id(1)
    seq = k_ref.shape[0]

    @pl.when(j == 0)
    def _():
        low = lax.broadcasted_iota(jnp.int32, (CHUNK, LANES), 1) < HEAD_DIM
        ones_low = jnp.where(low, 1.0, 0.0).astype(BF16)
        ones_high = jnp.where(low, 0.0, 1.0).astype(BF16)
        kpad_ref[0:CHUNK, :] = jnp.zeros((CHUNK, N_VARIANTS * LANES), BF16)
        kpad_ref[CHUNK + seq:, :] = jnp.zeros((CHUNK, N_VARIANTS * LANES), BF16)
        k_variants = _lane_half_variants(k_ref[...].astype(F32))
        v_variants = _lane_half_variants(v_ref[...].astype(F32))
        for vi in range(N_VARIANTS):
            kpad_ref[CHUNK:CHUNK + seq, vi * LANES:(vi + 1) * LANES] = k_variants[vi].astype(BF16)
            c0 = 2 * vi * LANES
            vpad_ref[0:CHUNK, c0:c0 + LANES] = jnp.zeros((CHUNK, LANES), BF16)
            vpad_ref[CHUNK:CHUNK + seq, c0:c0 + LANES] = v_variants[vi].astype(BF16)
            vpad_ref[CHUNK + seq:, c0:c0 + LANES] = jnp.zeros((CHUNK, LANES), BF16)
            ones = ones_low if vi % 2 == 0 else ones_high
            for r0 in range(0, seq + 2 * CHUNK, CHUNK):
                vpad_ref[r0:r0 + CHUNK, c0 + LANES:c0 + 2 * LANES] = ones

    blocks_per_step = x_ref.shape[0] // CHUNK
    first_rows = lax.broadcasted_iota(jnp.int32, (2 * CHUNK, 1), 0) < CHUNK
    low_lanes = lax.broadcasted_iota(jnp.int32, (2 * CHUNK, LANES), 1) < HEAD_DIM
    sink_cols = [[jnp.where(first_rows, sink_ref[layer, 4 * kvh + half], sink_ref[layer, 4 * kvh + half + 2])
                  for half in range(2)] for kvh in range(N_KV_HEADS)]

    def scores(i, kvh):
        n = j * blocks_per_step + i
        edge = jnp.where(n == 0, 0, jnp.where(n == n_blocks - 1, 2, 1))
        win = pl.ds(pl.multiple_of(n * CHUNK, CHUNK), HALF_W)
        r = slice(i * CHUNK, (i + 1) * CHUNK)
        cg = 2 * kvh
        lhs = jnp.concatenate([q_ref[r, cg * LANES:(cg + 1) * LANES],
                               q_ref[r, (cg + 1) * LANES:(cg + 2) * LANES]], axis=0)
        keys = jnp.concatenate([kpad_ref[win, kvh * KV_COLS:kvh * KV_COLS + LANES],
                                kpad_ref[win, kvh * KV_COLS + LANES:(kvh + 1) * KV_COLS]], axis=0)
        s = lax.dot_general(lhs, keys, (((1,), (1,)), ((), ())), preferred_element_type=F32)
        s = s + bias_ref[edge, kvh]
        m = [jnp.maximum(jnp.max(s[:, h * HALF_W:(h + 1) * HALF_W], axis=-1, keepdims=True),
                         sink_cols[kvh][h]) for h in range(2)]
        return s, m, win

    def outputs(kvh, s, m, win):
        e = jnp.concatenate([jnp.exp(s[:, h * HALF_W:(h + 1) * HALF_W] - m[h]) for h in range(2)],
                            axis=-1).astype(BF16)
        vals = jnp.concatenate([vpad_ref[win, 2 * kvh * KV_COLS:(2 * kvh + 1) * KV_COLS],
                                vpad_ref[win, (2 * kvh + 1) * KV_COLS:(2 * kvh + 2) * KV_COLS]], axis=0)
        pv = jnp.dot(e, vals, preferred_element_type=F32)
        e_sink = jnp.where(low_lanes, jnp.exp(sink_cols[kvh][0] - m[0]),
                           jnp.exp(sink_cols[kvh][1] - m[1]))
        o_kv = pv[:, :LANES] * (1.0 / (pv[:, LANES:] + e_sink))
        return [o_kv[0:CHUNK].astype(BF16), o_kv[CHUNK:].astype(BF16)]

    work = [(i, kvh) for i in range(blocks_per_step) for kvh in range(N_KV_HEADS)]
    tiles = []
    pending = scores(*work[0])
    for idx, (i, kvh) in enumerate(work):
        nxt = scores(*work[idx + 1]) if idx + 1 < len(work) else None
        tiles += outputs(kvh, *pending)
        pending = nxt
    per_block = 2 * N_KV_HEADS
    attn = jnp.concatenate([jnp.concatenate(tiles[b * per_block:(b + 1) * per_block], axis=-1)
                            for b in range(blocks_per_step)], axis=0)

    y = (jnp.dot(outa_ref[...], wout_ref[0:MIX_A, :], preferred_element_type=F32)
         + jnp.dot(attn, wout_ref[MIX_A:, :], preferred_element_type=F32))
    o_ref[...] = x_ref[...] + y


def _mix_out(x, outa, q, k, v, sink, bias_tab, w_out, layer, batch):
    t, d = x.shape
    seq = t // batch
    rows = MIX_OUT_ROWS
    steps = seq // rows
    resident = pl.Buffered(1)
    row_spec = lambda w: pl.BlockSpec((rows, w), lambda b, j: (b * steps + j, 0))
    seq_spec = pl.BlockSpec((seq, KV_W), lambda b, j: (b, 0))
    return pl.pallas_call(
        functools.partial(_mix_out_kernel, layer=layer, n_blocks=seq // CHUNK),
        grid=(batch, steps),
        in_specs=[
            pl.BlockSpec(memory_space=pltpu.SMEM),
            row_spec(d), row_spec(MIX_A), row_spec(MIX_B), seq_spec, seq_spec,
            pl.BlockSpec(bias_tab.shape, lambda b, j: (0, 0, 0, 0), pipeline_mode=resident),
            pl.BlockSpec((MIX_A + MIX_B, d), lambda b, j: (0, 0), pipeline_mode=resident),
        ],
        out_specs=row_spec(d),
        out_shape=jax.ShapeDtypeStruct((t, d), F32),
        scratch_shapes=[
            pltpu.VMEM((seq + 2 * CHUNK, N_VARIANTS * LANES), BF16),
            pltpu.VMEM((seq + 2 * CHUNK, 2 * N_VARIANTS * LANES), BF16),
        ],
        compiler_params=pltpu.CompilerParams(
            dimension_semantics=("parallel", "arbitrary"), vmem_limit_bytes=VMEM_LIMIT_BYTES),
        name="mix_out",
    )(sink, x, outa, q, k, v, bias_tab, w_out)


def _t5_bucket(rel):
    nb = N_BUCKETS // 2
    ret = (rel > 0).astype(np.int32) * nb
    n = np.abs(rel)
    max_exact = nb // 2
    large = max_exact + (np.log(np.maximum(n, 1).astype(np.float32) / max_exact)
                         / math.log(MAX_DISTANCE / max_exact) * (nb - max_exact)).astype(np.int32)
    large = np.minimum(large, nb - 1)
    return ret + np.where(n < max_exact, n, large).astype(np.int32)


def _bias_table(rel_bias):
    rel = np.arange(3 * CHUNK)[None, :] - CHUNK - np.arange(CHUNK)[:, None]
    bucket = _t5_bucket(rel)
    rb = rel_bias.astype(F32)
    bias = jnp.zeros((N_Q_HEADS,) + rel.shape, F32)
    for b in range(N_BUCKETS):
        bias = jnp.where((bucket == b)[None], rb[b][:, None, None], bias)
    band = np.abs(rel) <= CHUNK
    col = np.arange(3 * CHUNK)[None, :]
    edge_masks = [band & (col >= CHUNK), band, band & (col < 2 * CHUNK)]
    head_rows = np.array([[4 * (vi // 2) + (vi % 2), 4 * (vi // 2) + (vi % 2) + 2]
                          for vi in range(N_VARIANTS)])
    per_variant = jnp.stack([jnp.concatenate([bias[a], bias[b]], axis=0) for a, b in head_rows])
    tabs = [jnp.where(np.tile(mask, (2, 1))[None], per_variant, NEG_INF) for mask in edge_masks]
    tab = jnp.stack(tabs, axis=0)
    return jnp.concatenate([tab[:, 0::2], tab[:, 1::2]], axis=-1)


def kernel(x, ffn1_norm, ffn1_w_in, ffn1_w_out, mix_norm, w_mix_in, sgu_norm, sgu_w, sgu_b,
           q_norm, k_norm, sink, rel_bias, w_mix_out, ffn2_norm, ffn2_w_in, ffn2_w_out):
    batch, seq, d = x.shape
    depth = ffn1_norm.shape[0]
    assert seq % MIX_OUT_ROWS == 0 and (batch * seq) % FFN_ROWS == 0

    row3 = lambda a: a.astype(F32)[:, None, :]
    ffn1_g, mix_g, ffn2_g, sgu_g = row3(ffn1_norm), row3(mix_norm), row3(ffn2_norm), row3(sgu_norm)
    q_g = row3(jnp.tile(q_norm, (1, N_Q_HEADS)))
    k_g = row3(jnp.tile(k_norm, (1, N_KV_HEADS)))
    sgu_b3 = jnp.broadcast_to(sgu_b.astype(F32)[..., None], sgu_b.shape + (LANES,))
    ones_bd = jnp.asarray(np.kron(np.eye(MXU_DIM // HEAD_DIM), np.ones((HEAD_DIM, HEAD_DIM))), BF16)
    bias_tab = _bias_table(rel_bias)
    sink = sink.astype(F32)
    wi, wo = ffn1_w_in[0].astype(BF16), ffn1_w_out[0].astype(BF16)

    xt = x.reshape(batch * seq, d)
    for l in range(depth):
        xt, (mix_wi, mix_wo, wi, wo) = _ffn(
            xt, ffn1_g, wi, wo, l, [(w_mix_in, l), (w_mix_out, l), (ffn2_w_in, l), (ffn2_w_out, l)])
        outa, q, k, v = _mix_in(xt, mix_g, mix_wi, sgu_g, sgu_w, sgu_b3, q_g, k_g, ones_bd, l)
        xt = _mix_out(xt, outa, q, k, v, sink, bias_tab, mix_wo, l, batch)
        nxt = [(ffn1_w_in, l + 1), (ffn1_w_out, l + 1)] if l + 1 < depth else []
        xt, nxt_w = _ffn(xt, ffn2_g, wi, wo, l, nxt)
        if nxt:
            wi, wo = nxt_w
    return xt.reshape(batch, seq, d)
```

```python
import functools
import math

import jax
import jax.numpy as jnp
import numpy as np
from jax import lax
from jax.experimental import pallas as pl
from jax.experimental.pallas import tpu as pltpu

F32 = jnp.float32
BF16 = jnp.bfloat16

EPS = 1e-6
NEG_INF = -1e30

LANES = 128
MXU_DIM = 256
VMEM_LIMIT_BYTES = 56 * 1024 * 1024

SGU_GROUPS = 4
CHUNK = 128
N_Q_HEADS = 8
N_KV_HEADS = 2
HEAD_DIM = 64
N_BUCKETS = 32
MAX_DISTANCE = 128
MIX_A = SGU_GROUPS * LANES
MIX_B = N_Q_HEADS * HEAD_DIM
KV_W = N_KV_HEADS * HEAD_DIM
N_VARIANTS = 4

FFN_ROWS = 1024
FFN_CHUNK_TILES = (6, 5)
MIX_IN_ROWS = 1024
MIX_OUT_ROWS = 1024
OUT_PROJ_BLOCKS = 2


def _rms_scale(x, width):
    return lax.rsqrt(jnp.sum(x * x, axis=-1, keepdims=True) * (1.0 / width) + EPS)


def _ffn_kernel(*refs, d_ff, n_casts):
    x_ref, g_ref, win_ref, wout_ref = refs[:4]
    src_refs = refs[4:4 + n_casts]
    o_ref = refs[4 + n_casts]
    dst_refs = refs[5 + n_casts:]
    for src_ref, dst_ref in zip(src_refs, dst_refs):
        dst_ref[...] = src_ref[...].astype(BF16)
    x = x_ref[...]
    h = ((x * _rms_scale(x, x.shape[-1])) * g_ref[...]).astype(BF16)
    assert sum(FFN_CHUNK_TILES) * MXU_DIM == d_ff
    y = None
    lo = 0
    for tiles in FFN_CHUNK_TILES:
        tf = tiles * MXU_DIM
        gate = jnp.dot(h, win_ref[:, lo:lo + tf], preferred_element_type=F32)
        up = jnp.dot(h, win_ref[:, d_ff + lo:d_ff + lo + tf], preferred_element_type=F32)
        a = (gate * jax.nn.sigmoid(gate) * up).astype(BF16)
        part = jnp.dot(a, wout_ref[lo:lo + tf, :], preferred_element_type=F32)
        y = part if y is None else y + part
        lo += tf
    o_ref[...] = x + 0.5 * y


def _ffn(x, gain, w_in, w_out, layer, casts=()):
    t, d = x.shape
    d_ff = w_out.shape[0]
    steps = t // FFN_ROWS
    resident = pl.Buffered(1)
    cast_in_specs, cast_out_specs, cast_out_shapes = [], [], []
    for w, l in casts:
        _, rows, cols = w.shape
        slab = rows // steps
        assert slab * steps == rows and slab % 16 == 0
        cast_in_specs.append(pl.BlockSpec((None, slab, cols), lambda i, l=l: (l, i, 0)))
        cast_out_specs.append(pl.BlockSpec((slab, cols), lambda i: (i, 0)))
        cast_out_shapes.append(jax.ShapeDtypeStruct((rows, cols), BF16))
    outs = pl.pallas_call(
        functools.partial(_ffn_kernel, d_ff=d_ff, n_casts=len(casts)),
        grid=(steps,),
        in_specs=[
            pl.BlockSpec((FFN_ROWS, d), lambda i: (i, 0)),
            pl.BlockSpec((None, 1, d), lambda i: (layer, 0, 0)),
            pl.BlockSpec((d, 2 * d_ff), lambda i: (0, 0), pipeline_mode=resident),
            pl.BlockSpec((d_ff, d), lambda i: (0, 0), pipeline_mode=resident),
        ] + cast_in_specs,
        out_specs=[pl.BlockSpec((FFN_ROWS, d), lambda i: (i, 0))] + cast_out_specs,
        out_shape=[jax.ShapeDtypeStruct((t, d), F32)] + cast_out_shapes,
        compiler_params=pltpu.CompilerParams(
            dimension_semantics=("parallel",), vmem_limit_bytes=VMEM_LIMIT_BYTES),
        name="ffn",
    )(x, gain, w_in, w_out, *[w for w, _ in casts])
    return outs[0], outs[1:]


def _segment_sumsq(x):
    x2 = x * x
    low = lax.broadcasted_iota(jnp.int32, (x.shape[0], LANES), 1) < HEAD_DIM
    parts = []
    for c in range(0, x.shape[-1], LANES):
        t = x2[:, c:c + LANES]
        s_low = jnp.sum(jnp.where(low, t, 0.0), axis=-1, keepdims=True)
        s_high = jnp.sum(jnp.where(low, 0.0, t), axis=-1, keepdims=True)
        parts.append(jnp.where(low, s_low, s_high))
    return parts[0] if len(parts) == 1 else jnp.concatenate(parts, axis=-1)


def _lane_half_variants(x):
    swapped = pltpu.roll(x, HEAD_DIM, axis=1)
    low = lax.broadcasted_iota(jnp.int32, x.shape, 1) < HEAD_DIM
    zero = jnp.zeros_like(x)
    return [
        jnp.where(low, x, zero),
        jnp.where(low, zero, swapped),
        jnp.where(low, swapped, zero),
        jnp.where(low, zero, x),
    ]


def _mix_in_kernel(x_ref, g_ref, win_ref, sgu_g_ref, sgu_w_ref, sgu_b_ref, qg_ref, kg_ref,
                   outa_ref, q_ref, k_ref, v_ref):
    x = x_ref[...]
    rows = x.shape[0]
    h = ((x * _rms_scale(x, x.shape[-1])) * g_ref[...]).astype(BF16)
    p = jnp.dot(h, win_ref[...], preferred_element_type=F32)

    z = jax.nn.gelu(p[:, :2 * MIX_A])
    for g in range(SGU_GROUPS):
        u = z[:, g * LANES:(g + 1) * LANES]
        v = z[:, MIX_A + g * LANES:MIX_A + (g + 1) * LANES]
        vn = ((v * _rms_scale(v, LANES)) * sgu_g_ref[:, g * LANES:(g + 1) * LANES]).astype(BF16)
        w_s = sgu_w_ref[g].astype(BF16)
        b_s = sgu_b_ref[g]
        chunks = [slice(c * CHUNK, (c + 1) * CHUNK) for c in range(rows // CHUNK)]
        s = jnp.dot(w_s, jnp.concatenate([vn[r] for r in chunks], axis=1),
                    preferred_element_type=F32)
        for c, r in enumerate(chunks):
            outa_ref[r, g * LANES:(g + 1) * LANES] = (
                u[r] * (s[:, c * LANES:(c + 1) * LANES] + b_s)).astype(BF16)

    q = p[:, 2 * MIX_A:2 * MIX_A + MIX_B]
    q_inv = lax.rsqrt(_segment_sumsq(q) * (1.0 / HEAD_DIM) + EPS)
    q_ref[...] = (((q * q_inv) * qg_ref[...]) * (HEAD_DIM ** -0.5)).astype(BF16)

    k = p[:, 2 * MIX_A + MIX_B:2 * MIX_A + MIX_B + KV_W]
    k_inv = lax.rsqrt(_segment_sumsq(k) * (1.0 / HEAD_DIM) + EPS)
    k_ref[...] = ((k * k_inv) * kg_ref[...]).astype(BF16)
    v_ref[...] = p[:, 2 * MIX_A + MIX_B + KV_W:].astype(BF16)


def _mix_in(x, gain, w_in, sgu_g, sgu_w, sgu_b, q_g, k_g, layer):
    t, d = x.shape
    in_cols = w_in.shape[-1]
    rows = MIX_IN_ROWS
    row_spec = lambda w: pl.BlockSpec((rows, w), lambda i: (i, 0))
    layer_spec = lambda *shape: pl.BlockSpec((None,) + shape, lambda i: (layer,) + (0,) * len(shape))
    return pl.pallas_call(
        _mix_in_kernel,
        grid=(t // rows,),
        in_specs=[
            row_spec(d),
            layer_spec(1, d),
            pl.BlockSpec((d, in_cols), lambda i: (0, 0)),
            layer_spec(1, MIX_A),
            layer_spec(SGU_GROUPS, CHUNK, CHUNK),
            layer_spec(SGU_GROUPS, CHUNK, LANES),
            layer_spec(1, MIX_B),
            layer_spec(1, KV_W),
        ],
        out_specs=[row_spec(MIX_A), row_spec(MIX_B), row_spec(KV_W), row_spec(KV_W)],
        out_shape=[
            jax.ShapeDtypeStruct((t, MIX_A), BF16),
            jax.ShapeDtypeStruct((t, MIX_B), BF16),
            jax.ShapeDtypeStruct((t, KV_W), BF16),
            jax.ShapeDtypeStruct((t, KV_W), BF16),
        ],
        compiler_params=pltpu.CompilerParams(
            dimension_semantics=("parallel",), vmem_limit_bytes=VMEM_LIMIT_BYTES),
        name="mix_in",
    )(x, gain, w_in, sgu_g, sgu_w, sgu_b, q_g, k_g)


HALF_W = 3 * CHUNK
KV_COLS = 2 * LANES


def _mix_out_kernel(sink_ref, x_ref, outa_ref, q_ref, k_ref, v_ref, bias_ref, wout_ref,
                    o_ref, kpad_ref, vpad_ref, *, layer, n_blocks):
    j = pl.program_id(1)
    seq = k_ref.shape[0]

    @pl.when(j == 0)
    def _():
        low = lax.broadcasted_iota(jnp.int32, (CHUNK, LANES), 1) < HEAD_DIM
        ones_low = jnp.where(low, 1.0, 0.0).astype(BF16)
        ones_high = jnp.where(low, 0.0, 1.0).astype(BF16)
        kpad_ref[0:CHUNK, :] = jnp.zeros((CHUNK, N_VARIANTS * LANES), BF16)
        kpad_ref[CHUNK + seq:, :] = jnp.zeros((CHUNK, N_VARIANTS * LANES), BF16)
        k_variants = _lane_half_variants(k_ref[...].astype(F32))
        v_variants = _lane_half_variants(v_ref[...].astype(F32))
        for vi in range(N_VARIANTS):
            kpad_ref[CHUNK:CHUNK + seq, vi * LANES:(vi + 1) * LANES] = k_variants[vi].astype(BF16)
            c0 = 2 * vi * LANES
            vpad_ref[0:CHUNK, c0:c0 + LANES] = jnp.zeros((CHUNK, LANES), BF16)
            vpad_ref[CHUNK:CHUNK + seq, c0:c0 + LANES] = v_variants[vi].astype(BF16)
            vpad_ref[CHUNK + seq:, c0:c0 + LANES] = jnp.zeros((CHUNK, LANES), BF16)
            ones = ones_low if vi % 2 == 0 else ones_high
            for r0 in range(0, seq + 2 * CHUNK, CHUNK):
                vpad_ref[r0:r0 + CHUNK, c0 + LANES:c0 + 2 * LANES] = ones

    blocks_per_step = x_ref.shape[0] // CHUNK
    first_rows = lax.broadcasted_iota(jnp.int32, (2 * CHUNK, 1), 0) < CHUNK
    low_lanes = lax.broadcasted_iota(jnp.int32, (2 * CHUNK, LANES), 1) < HEAD_DIM
    sink_cols = [[jnp.where(first_rows, sink_ref[layer, 4 * kvh + half], sink_ref[layer, 4 * kvh + half + 2])
                  for half in range(2)] for kvh in range(N_KV_HEADS)]

    def scores(i, kvh):
        n = j * blocks_per_step + i
        edge = jnp.where(n == 0, 0, jnp.where(n == n_blocks - 1, 2, 1))
        win = pl.ds(pl.multiple_of(n * CHUNK, CHUNK), HALF_W)
        r = slice(i * CHUNK, (i + 1) * CHUNK)
        cg = 2 * kvh
        lhs = jnp.concatenate([q_ref[r, cg * LANES:(cg + 1) * LANES],
                               q_ref[r, (cg + 1) * LANES:(cg + 2) * LANES]], axis=0)
        keys = jnp.concatenate([kpad_ref[win, kvh * KV_COLS:kvh * KV_COLS + LANES],
                                kpad_ref[win, kvh * KV_COLS + LANES:(kvh + 1) * KV_COLS]], axis=0)
        s = lax.dot_general(lhs, keys, (((1,), (1,)), ((), ())), preferred_element_type=F32)
        s = s + bias_ref[edge, kvh]
        m = [jnp.maximum(jnp.max(s[:, h * HALF_W:(h + 1) * HALF_W], axis=-1, keepdims=True),
                         sink_cols[kvh][h]) for h in range(2)]
        return s, m, win

    def outputs(kvh, s, m, win):
        e = jnp.concatenate([jnp.exp(s[:, h * HALF_W:(h + 1) * HALF_W] - m[h]) for h in range(2)],
                            axis=-1).astype(BF16)
        vals = jnp.concatenate([vpad_ref[win, 2 * kvh * KV_COLS:(2 * kvh + 1) * KV_COLS],
                                vpad_ref[win, (2 * kvh + 1) * KV_COLS:(2 * kvh + 2) * KV_COLS]], axis=0)
        pv = jnp.dot(e, vals, preferred_element_type=F32)
        e_sink = jnp.where(low_lanes, jnp.exp(sink_cols[kvh][0] - m[0]),
                           jnp.exp(sink_cols[kvh][1] - m[1]))
        o_kv = pv[:, :LANES] * (1.0 / (pv[:, LANES:] + e_sink))
        return [o_kv[0:CHUNK].astype(BF16), o_kv[CHUNK:].astype(BF16)]

    work = [(i, kvh) for i in range(blocks_per_step) for kvh in range(N_KV_HEADS)]
    per_block = 2 * N_KV_HEADS
    tiles = []
    proj_pieces = []

    def proj_piece(rows, mixed, c):
        cols = slice(c * MXU_DIM, (c + 1) * MXU_DIM)
        o_ref[rows, cols] = x_ref[rows, cols] + jnp.dot(mixed, wout_ref[:, cols],
                                                        preferred_element_type=F32)

    pending = scores(*work[0])
    for idx, (i, kvh) in enumerate(work):
        nxt = scores(*work[idx + 1]) if idx + 1 < len(work) else None
        tiles += outputs(kvh, *pending)
        pending = nxt
        if proj_pieces:
            proj_piece(*proj_pieces.pop(0))
        if kvh == N_KV_HEADS - 1 and (i + 1) % OUT_PROJ_BLOCKS == 0:
            rows = slice((i + 1 - OUT_PROJ_BLOCKS) * CHUNK, (i + 1) * CHUNK)
            attn = jnp.concatenate(
                [jnp.concatenate(tiles[b * per_block:(b + 1) * per_block], axis=-1)
                 for b in range(OUT_PROJ_BLOCKS)], axis=0)
            tiles = []
            mixed = jnp.concatenate([outa_ref[rows, :], attn], axis=-1)
            proj_pieces += [(rows, mixed, c) for c in range(o_ref.shape[1] // MXU_DIM)]
    for piece in proj_pieces:
        proj_piece(*piece)


def _mix_out(x, outa, q, k, v, sink, bias_tab, w_out, layer, batch):
    t, d = x.shape
    seq = t // batch
    rows = MIX_OUT_ROWS
    steps = seq // rows
    resident = pl.Buffered(1)
    row_spec = lambda w: pl.BlockSpec((rows, w), lambda b, j: (b * steps + j, 0))
    seq_spec = pl.BlockSpec((seq, KV_W), lambda b, j: (b, 0))
    return pl.pallas_call(
        functools.partial(_mix_out_kernel, layer=layer, n_blocks=seq // CHUNK),
        grid=(batch, steps),
        in_specs=[
            pl.BlockSpec(memory_space=pltpu.SMEM),
            row_spec(d), row_spec(MIX_A), row_spec(MIX_B), seq_spec, seq_spec,
            pl.BlockSpec(bias_tab.shape, lambda b, j: (0, 0, 0, 0), pipeline_mode=resident),
            pl.BlockSpec((MIX_A + MIX_B, d), lambda b, j: (0, 0), pipeline_mode=resident),
        ],
        out_specs=row_spec(d),
        out_shape=jax.ShapeDtypeStruct((t, d), F32),
        scratch_shapes=[
            pltpu.VMEM((seq + 2 * CHUNK, N_VARIANTS * LANES), BF16),
            pltpu.VMEM((seq + 2 * CHUNK, 2 * N_VARIANTS * LANES), BF16),
        ],
        compiler_params=pltpu.CompilerParams(
            dimension_semantics=("parallel", "arbitrary"), vmem_limit_bytes=VMEM_LIMIT_BYTES),
        name="mix_out",
    )(sink, x, outa, q, k, v, bias_tab, w_out)


def _t5_bucket(rel):
    nb = N_BUCKETS // 2
    ret = (rel > 0).astype(np.int32) * nb
    n = np.abs(rel)
    max_exact = nb // 2
    large = max_exact + (np.log(np.maximum(n, 1).astype(np.float32) / max_exact)
                         / math.log(MAX_DISTANCE / max_exact) * (nb - max_exact)).astype(np.int32)
    large = np.minimum(large, nb - 1)
    return ret + np.where(n < max_exact, n, large).astype(np.int32)


def _bias_table(rel_bias):
    rel = np.arange(3 * CHUNK)[None, :] - CHUNK - np.arange(CHUNK)[:, None]
    bucket = _t5_bucket(rel)
    rb = rel_bias.astype(F32)
    bias = jnp.zeros((N_Q_HEADS,) + rel.shape, F32)
    for b in range(N_BUCKETS):
        bias = jnp.where((bucket == b)[None], rb[b][:, None, None], bias)
    band = np.abs(rel) <= CHUNK
    col = np.arange(3 * CHUNK)[None, :]
    edge_masks = [band & (col >= CHUNK), band, band & (col < 2 * CHUNK)]
    head_rows = np.array([[4 * (vi // 2) + (vi % 2), 4 * (vi // 2) + (vi % 2) + 2]
                          for vi in range(N_VARIANTS)])
    per_variant = jnp.stack([jnp.concatenate([bias[a], bias[b]], axis=0) for a, b in head_rows])
    tabs = [jnp.where(np.tile(mask, (2, 1))[None], per_variant, NEG_INF) for mask in edge_masks]
    tab = jnp.stack(tabs, axis=0)
    return jnp.concatenate([tab[:, 0::2], tab[:, 1::2]], axis=-1)


def kernel(x, ffn1_norm, ffn1_w_in, ffn1_w_out, mix_norm, w_mix_in, sgu_norm, sgu_w, sgu_b,
           q_norm, k_norm, sink, rel_bias, w_mix_out, ffn2_norm, ffn2_w_in, ffn2_w_out):
    batch, seq, d = x.shape
    depth = ffn1_norm.shape[0]
    assert seq % MIX_OUT_ROWS == 0 and (batch * seq) % FFN_ROWS == 0

    row3 = lambda a: a.astype(F32)[:, None, :]
    ffn1_g, mix_g, ffn2_g, sgu_g = row3(ffn1_norm), row3(mix_norm), row3(ffn2_norm), row3(sgu_norm)
    q_g = row3(jnp.tile(q_norm, (1, N_Q_HEADS)))
    k_g = row3(jnp.tile(k_norm, (1, N_KV_HEADS)))
    sgu_b3 = jnp.broadcast_to(sgu_b.astype(F32)[..., None], sgu_b.shape + (LANES,))
    bias_tab = _bias_table(rel_bias)
    sink = sink.astype(F32)
    wi, wo = ffn1_w_in[0].astype(BF16), ffn1_w_out[0].astype(BF16)

    xt = x.reshape(batch * seq, d)
    for l in range(depth):
        xt, (mix_wi, mix_wo, wi, wo) = _ffn(
            xt, ffn1_g, wi, wo, l, [(w_mix_in, l), (w_mix_out, l), (ffn2_w_in, l), (ffn2_w_out, l)])
        outa, q, k, v = _mix_in(xt, mix_g, mix_wi, sgu_g, sgu_w, sgu_b3, q_g, k_g, l)
        xt = _mix_out(xt, outa, q, k, v, sink, bias_tab, mix_wo, l, batch)
        nxt = [(ffn1_w_in, l + 1), (ffn1_w_out, l + 1)] if l + 1 < depth else []
        xt, nxt_w = _ffn(xt, ffn2_g, wi, wo, l, nxt)
        if nxt:
            wi, wo = nxt_w
    return xt.reshape(batch, seq, d)
```

```python
import functools
import math

import jax
import jax.numpy as jnp
import numpy as np
from jax import lax
from jax.experimental import pallas as pl
from jax.experimental.pallas import tpu as pltpu

F32 = jnp.float32
BF16 = jnp.bfloat16

EPS = 1e-6
NEG_INF = -1e30

LANES = 128
MXU_DIM = 256
VMEM_LIMIT_BYTES = 56 * 1024 * 1024

SGU_GROUPS = 4
CHUNK = 128
N_Q_HEADS = 8
N_KV_HEADS = 2
HEAD_DIM = 64
N_BUCKETS = 32
MAX_DISTANCE = 128
MIX_A = SGU_GROUPS * LANES
MIX_B = N_Q_HEADS * HEAD_DIM
KV_W = N_KV_HEADS * HEAD_DIM
N_VARIANTS = 4

FFN_ROWS = 1024
FFN_CHUNK_TILES = (6, 5)
MIX_IN_ROWS = 1024
MIX_OUT_ROWS = 1024
OUT_PROJ_BLOCKS = 2


def _rms_scale(x, width):
    return lax.rsqrt(jnp.sum(x * x, axis=-1, keepdims=True) * (1.0 / width) + EPS)


def _ffn_kernel(*refs, d_ff, n_casts):
    x_ref, g_ref, win_ref, wout_ref = refs[:4]
    src_refs = refs[4:4 + n_casts]
    o_ref = refs[4 + n_casts]
    dst_refs = refs[5 + n_casts:]
    for src_ref, dst_ref in zip(src_refs, dst_refs):
        dst_ref[...] = src_ref[...].astype(BF16)
    x = x_ref[...]
    h = ((x * _rms_scale(x, x.shape[-1])) * g_ref[...]).astype(BF16)
    assert sum(FFN_CHUNK_TILES) * MXU_DIM == d_ff
    y = None
    lo = 0
    for tiles in FFN_CHUNK_TILES:
        tf = tiles * MXU_DIM
        gate = jnp.dot(h, win_ref[:, lo:lo + tf], preferred_element_type=F32)
        up = jnp.dot(h, win_ref[:, d_ff + lo:d_ff + lo + tf], preferred_element_type=F32)
        a = (gate * jax.nn.sigmoid(gate) * up).astype(BF16)
        part = jnp.dot(a, wout_ref[lo:lo + tf, :], preferred_element_type=F32)
        y = part if y is None else y + part
        lo += tf
    o_ref[...] = x + 0.5 * y


def _ffn(x, gain, w_in, w_out, layer, casts=()):
    t, d = x.shape
    d_ff = w_out.shape[0]
    steps = t // FFN_ROWS
    resident = pl.Buffered(1)
    cast_in_specs, cast_out_specs, cast_out_shapes = [], [], []
    for w, l in casts:
        _, rows, cols = w.shape
        slab = rows // steps
        assert slab * steps == rows and slab % 16 == 0
        cast_in_specs.append(pl.BlockSpec((None, slab, cols), lambda i, l=l: (l, i, 0)))
        cast_out_specs.append(pl.BlockSpec((slab, cols), lambda i: (i, 0)))
        cast_out_shapes.append(jax.ShapeDtypeStruct((rows, cols), BF16))
    outs = pl.pallas_call(
        functools.partial(_ffn_kernel, d_ff=d_ff, n_casts=len(casts)),
        grid=(steps,),
        in_specs=[
            pl.BlockSpec((FFN_ROWS, d), lambda i: (i, 0)),
            pl.BlockSpec((None, 1, d), lambda i: (layer, 0, 0)),
            pl.BlockSpec((d, 2 * d_ff), lambda i: (0, 0), pipeline_mode=resident),
            pl.BlockSpec((d_ff, d), lambda i: (0, 0), pipeline_mode=resident),
        ] + cast_in_specs,
        out_specs=[pl.BlockSpec((FFN_ROWS, d), lambda i: (i, 0))] + cast_out_specs,
        out_shape=[jax.ShapeDtypeStruct((t, d), F32)] + cast_out_shapes,
        compiler_params=pltpu.CompilerParams(
            dimension_semantics=("parallel",), vmem_limit_bytes=VMEM_LIMIT_BYTES),
        name="ffn",
    )(x, gain, w_in, w_out, *[w for w, _ in casts])
    return outs[0], outs[1:]


def _segment_sumsq(x):
    x2 = x * x
    low = lax.broadcasted_iota(jnp.int32, (x.shape[0], LANES), 1) < HEAD_DIM
    parts = []
    for c in range(0, x.shape[-1], LANES):
        t = x2[:, c:c + LANES]
        s_low = jnp.sum(jnp.where(low, t, 0.0), axis=-1, keepdims=True)
        s_high = jnp.sum(jnp.where(low, 0.0, t), axis=-1, keepdims=True)
        parts.append(jnp.where(low, s_low, s_high))
    return parts[0] if len(parts) == 1 else jnp.concatenate(parts, axis=-1)


def _lane_half_variants(x):
    packed = pltpu.bitcast(x, jnp.uint32)
    swapped = pltpu.bitcast(pltpu.roll(packed, HEAD_DIM, axis=1), x.dtype)
    low = lax.broadcasted_iota(jnp.int32, x.shape, 1) < HEAD_DIM
    zero = jnp.zeros_like(x)
    return [
        jnp.where(low, x, zero),
        jnp.where(low, zero, swapped),
        jnp.where(low, swapped, zero),
        jnp.where(low, zero, x),
    ]


def _mix_in_kernel(x_ref, g_ref, win_ref, sgu_g_ref, sgu_w_ref, sgu_b_ref, qg_ref, kg_ref,
                   outa_ref, q_ref, k_ref, v_ref):
    x = x_ref[...]
    rows = x.shape[0]
    h = ((x * _rms_scale(x, x.shape[-1])) * g_ref[...]).astype(BF16)
    p = jnp.dot(h, win_ref[...], preferred_element_type=F32)

    z = jax.nn.gelu(p[:, :2 * MIX_A])
    for g in range(SGU_GROUPS):
        u = z[:, g * LANES:(g + 1) * LANES]
        v = z[:, MIX_A + g * LANES:MIX_A + (g + 1) * LANES]
        vn = ((v * _rms_scale(v, LANES)) * sgu_g_ref[:, g * LANES:(g + 1) * LANES]).astype(BF16)
        w_s = sgu_w_ref[g].astype(BF16)
        b_s = sgu_b_ref[g]
        chunks = [slice(c * CHUNK, (c + 1) * CHUNK) for c in range(rows // CHUNK)]
        s = jnp.dot(w_s, jnp.concatenate([vn[r] for r in chunks], axis=1),
                    preferred_element_type=F32)
        for c, r in enumerate(chunks):
            outa_ref[r, g * LANES:(g + 1) * LANES] = (
                u[r] * (s[:, c * LANES:(c + 1) * LANES] + b_s)).astype(BF16)

    q = p[:, 2 * MIX_A:2 * MIX_A + MIX_B]
    q_inv = lax.rsqrt(_segment_sumsq(q) * (1.0 / HEAD_DIM) + EPS)
    q_ref[...] = (((q * q_inv) * qg_ref[...]) * (HEAD_DIM ** -0.5)).astype(BF16)

    k = p[:, 2 * MIX_A + MIX_B:2 * MIX_A + MIX_B + KV_W]
    k_inv = lax.rsqrt(_segment_sumsq(k) * (1.0 / HEAD_DIM) + EPS)
    k_ref[...] = ((k * k_inv) * kg_ref[...]).astype(BF16)
    v_ref[...] = p[:, 2 * MIX_A + MIX_B + KV_W:].astype(BF16)


def _mix_in(x, gain, w_in, sgu_g, sgu_w, sgu_b, q_g, k_g, layer):
    t, d = x.shape
    in_cols = w_in.shape[-1]
    rows = MIX_IN_ROWS
    row_spec = lambda w: pl.BlockSpec((rows, w), lambda i: (i, 0))
    layer_spec = lambda *shape: pl.BlockSpec((None,) + shape, lambda i: (layer,) + (0,) * len(shape))
    return pl.pallas_call(
        _mix_in_kernel,
        grid=(t // rows,),
        in_specs=[
            row_spec(d),
            layer_spec(1, d),
            pl.BlockSpec((d, in_cols), lambda i: (0, 0), pipeline_mode=pl.Buffered(1)),
            layer_spec(1, MIX_A),
            layer_spec(SGU_GROUPS, CHUNK, CHUNK),
            layer_spec(SGU_GROUPS, CHUNK, LANES),
            layer_spec(1, MIX_B),
            layer_spec(1, KV_W),
        ],
        out_specs=[row_spec(MIX_A), row_spec(MIX_B), row_spec(KV_W), row_spec(KV_W)],
        out_shape=[
            jax.ShapeDtypeStruct((t, MIX_A), BF16),
            jax.ShapeDtypeStruct((t, MIX_B), BF16),
            jax.ShapeDtypeStruct((t, KV_W), BF16),
            jax.ShapeDtypeStruct((t, KV_W), BF16),
        ],
        compiler_params=pltpu.CompilerParams(
            dimension_semantics=("parallel",), vmem_limit_bytes=VMEM_LIMIT_BYTES),
        name="mix_in",
    )(x, gain, w_in, sgu_g, sgu_w, sgu_b, q_g, k_g)


HALF_W = 3 * CHUNK
KV_COLS = 2 * LANES


def _mix_out_kernel(sink_ref, x_ref, outa_ref, q_ref, k_ref, v_ref, bias_ref, wout_ref,
                    o_ref, kpad_ref, vpad_ref, *, layer, n_blocks):
    j = pl.program_id(1)
    seq = k_ref.shape[0]

    @pl.when(j == 0)
    def _():
        low = lax.broadcasted_iota(jnp.int32, (CHUNK, LANES), 1) < HEAD_DIM
        ones_low = jnp.where(low, 1.0, 0.0).astype(BF16)
        ones_high = jnp.where(low, 0.0, 1.0).astype(BF16)
        kpad_ref[0:CHUNK, :] = jnp.zeros((CHUNK, N_VARIANTS * LANES), BF16)
        kpad_ref[CHUNK + seq:, :] = jnp.zeros((CHUNK, N_VARIANTS * LANES), BF16)
        k_variants = _lane_half_variants(k_ref[...])
        v_variants = _lane_half_variants(v_ref[...])
        for vi in range(N_VARIANTS):
            kpad_ref[CHUNK:CHUNK + seq, vi * LANES:(vi + 1) * LANES] = k_variants[vi]
            c0 = 2 * vi * LANES
            vpad_ref[0:CHUNK, c0:c0 + LANES] = jnp.zeros((CHUNK, LANES), BF16)
            vpad_ref[CHUNK:CHUNK + seq, c0:c0 + LANES] = v_variants[vi]
            vpad_ref[CHUNK + seq:, c0:c0 + LANES] = jnp.zeros((CHUNK, LANES), BF16)
            ones = ones_low if vi % 2 == 0 else ones_high
            for r0 in range(0, seq + 2 * CHUNK, CHUNK):
                vpad_ref[r0:r0 + CHUNK, c0 + LANES:c0 + 2 * LANES] = ones

    blocks_per_step = x_ref.shape[0] // CHUNK
    first_rows = lax.broadcasted_iota(jnp.int32, (2 * CHUNK, 1), 0) < CHUNK
    low_lanes = lax.broadcasted_iota(jnp.int32, (2 * CHUNK, LANES), 1) < HEAD_DIM
    sink_cols = [[jnp.where(first_rows, sink_ref[layer, 4 * kvh + half], sink_ref[layer, 4 * kvh + half + 2])
                  for half in range(2)] for kvh in range(N_KV_HEADS)]

    def scores(i, kvh):
        n = j * blocks_per_step + i
        edge = jnp.where(n == 0, 0, jnp.where(n == n_blocks - 1, 2, 1))
        win = pl.ds(pl.multiple_of(n * CHUNK, CHUNK), HALF_W)
        r = slice(i * CHUNK, (i + 1) * CHUNK)
        cg = 2 * kvh
        lhs = jnp.concatenate([q_ref[r, cg * LANES:(cg + 1) * LANES],
                               q_ref[r, (cg + 1) * LANES:(cg + 2) * LANES]], axis=0)
        keys = jnp.concatenate([kpad_ref[win, kvh * KV_COLS:kvh * KV_COLS + LANES],
                                kpad_ref[win, kvh * KV_COLS + LANES:(kvh + 1) * KV_COLS]], axis=0)
        s = lax.dot_general(lhs, keys, (((1,), (1,)), ((), ())), preferred_element_type=F32)
        s = s + bias_ref[edge, kvh]
        m = [jnp.maximum(jnp.max(s[:, h * HALF_W:(h + 1) * HALF_W], axis=-1, keepdims=True),
                         sink_cols[kvh][h]) for h in range(2)]
        return s, m, win

    def outputs(kvh, s, m, win):
        e = jnp.concatenate([jnp.exp(s[:, h * HALF_W:(h + 1) * HALF_W] - m[h]) for h in range(2)],
                            axis=-1).astype(BF16)
        vals = jnp.concatenate([vpad_ref[win, 2 * kvh * KV_COLS:(2 * kvh + 1) * KV_COLS],
                                vpad_ref[win, (2 * kvh + 1) * KV_COLS:(2 * kvh + 2) * KV_COLS]], axis=0)
        pv = jnp.dot(e, vals, preferred_element_type=F32)
        e_sink = jnp.where(low_lanes, jnp.exp(sink_cols[kvh][0] - m[0]),
                           jnp.exp(sink_cols[kvh][1] - m[1]))
        o_kv = pv[:, :LANES] * (1.0 / (pv[:, LANES:] + e_sink))
        return [o_kv[0:CHUNK].astype(BF16), o_kv[CHUNK:].astype(BF16)]

    work = [(i, kvh) for i in range(blocks_per_step) for kvh in range(N_KV_HEADS)]
    per_block = 2 * N_KV_HEADS
    tiles = []
    proj_pieces = []

    def proj_piece(rows, mixed, c):
        cols = slice(c * MXU_DIM, (c + 1) * MXU_DIM)
        o_ref[rows, cols] = x_ref[rows, cols] + jnp.dot(mixed, wout_ref[:, cols],
                                                        preferred_element_type=F32)

    pending = scores(*work[0])
    for idx, (i, kvh) in enumerate(work):
        nxt = scores(*work[idx + 1]) if idx + 1 < len(work) else None
        tiles += outputs(kvh, *pending)
        pending = nxt
        if proj_pieces:
            proj_piece(*proj_pieces.pop(0))
        if kvh == N_KV_HEADS - 1 and (i + 1) % OUT_PROJ_BLOCKS == 0:
            rows = slice((i + 1 - OUT_PROJ_BLOCKS) * CHUNK, (i + 1) * CHUNK)
            attn = jnp.concatenate(
                [jnp.concatenate(tiles[b * per_block:(b + 1) * per_block], axis=-1)
                 for b in range(OUT_PROJ_BLOCKS)], axis=0)
            tiles = []
            mixed = jnp.concatenate([outa_ref[rows, :], attn], axis=-1)
            proj_pieces += [(rows, mixed, c) for c in range(o_ref.shape[1] // MXU_DIM)]
    for piece in proj_pieces:
        proj_piece(*piece)


def _mix_out(x, outa, q, k, v, sink, bias_tab, w_out, layer, batch):
    t, d = x.shape
    seq = t // batch
    rows = MIX_OUT_ROWS
    steps = seq // rows
    resident = pl.Buffered(1)
    row_spec = lambda w: pl.BlockSpec((rows, w), lambda b, j: (b * steps + j, 0))
    seq_spec = pl.BlockSpec((seq, KV_W), lambda b, j: (b, 0))
    return pl.pallas_call(
        functools.partial(_mix_out_kernel, layer=layer, n_blocks=seq // CHUNK),
        grid=(batch, steps),
        in_specs=[
            pl.BlockSpec(memory_space=pltpu.SMEM),
            row_spec(d), row_spec(MIX_A), row_spec(MIX_B), seq_spec, seq_spec,
            pl.BlockSpec(bias_tab.shape, lambda b, j: (0, 0, 0, 0), pipeline_mode=resident),
            pl.BlockSpec((MIX_A + MIX_B, d), lambda b, j: (0, 0), pipeline_mode=resident),
        ],
        out_specs=row_spec(d),
        out_shape=jax.ShapeDtypeStruct((t, d), F32),
        scratch_shapes=[
            pltpu.VMEM((seq + 2 * CHUNK, N_VARIANTS * LANES), BF16),
            pltpu.VMEM((seq + 2 * CHUNK, 2 * N_VARIANTS * LANES), BF16),
        ],
        compiler_params=pltpu.CompilerParams(
            dimension_semantics=("parallel", "arbitrary"), vmem_limit_bytes=VMEM_LIMIT_BYTES),
        name="mix_out",
    )(sink, x, outa, q, k, v, bias_tab, w_out)


def _t5_bucket(rel):
    nb = N_BUCKETS // 2
    ret = (rel > 0).astype(np.int32) * nb
    n = np.abs(rel)
    max_exact = nb // 2
    large = max_exact + (np.log(np.maximum(n, 1).astype(np.float32) / max_exact)
                         / math.log(MAX_DISTANCE / max_exact) * (nb - max_exact)).astype(np.int32)
    large = np.minimum(large, nb - 1)
    return ret + np.where(n < max_exact, n, large).astype(np.int32)


def _bias_table(rel_bias):
    rel = np.arange(3 * CHUNK)[None, :] - CHUNK - np.arange(CHUNK)[:, None]
    bucket = _t5_bucket(rel)
    rb = rel_bias.astype(F32)
    bias = jnp.zeros((N_Q_HEADS,) + rel.shape, F32)
    for b in range(N_BUCKETS):
        bias = jnp.where((bucket == b)[None], rb[b][:, None, None], bias)
    band = np.abs(rel) <= CHUNK
    col = np.arange(3 * CHUNK)[None, :]
    edge_masks = [band & (col >= CHUNK), band, band & (col < 2 * CHUNK)]
    head_rows = np.array([[4 * (vi // 2) + (vi % 2), 4 * (vi // 2) + (vi % 2) + 2]
                          for vi in range(N_VARIANTS)])
    per_variant = jnp.stack([jnp.concatenate([bias[a], bias[b]], axis=0) for a, b in head_rows])
    tabs = [jnp.where(np.tile(mask, (2, 1))[None], per_variant, NEG_INF) for mask in edge_masks]
    tab = jnp.stack(tabs, axis=0)
    return jnp.concatenate([tab[:, 0::2], tab[:, 1::2]], axis=-1)


def kernel(x, ffn1_norm, ffn1_w_in, ffn1_w_out, mix_norm, w_mix_in, sgu_norm, sgu_w, sgu_b,
           q_norm, k_norm, sink, rel_bias, w_mix_out, ffn2_norm, ffn2_w_in, ffn2_w_out):
    batch, seq, d = x.shape
    depth = ffn1_norm.shape[0]
    assert seq % MIX_OUT_ROWS == 0 and (batch * seq) % FFN_ROWS == 0

    row3 = lambda a: a.astype(F32)[:, None, :]
    ffn1_g, mix_g, ffn2_g, sgu_g = row3(ffn1_norm), row3(mix_norm), row3(ffn2_norm), row3(sgu_norm)
    q_g = row3(jnp.tile(q_norm, (1, N_Q_HEADS)))
    k_g = row3(jnp.tile(k_norm, (1, N_KV_HEADS)))
    sgu_b3 = jnp.broadcast_to(sgu_b.astype(F32)[..., None], sgu_b.shape + (LANES,))
    bias_tab = _bias_table(rel_bias)
    sink = sink.astype(F32)
    wi, wo = ffn1_w_in[0].astype(BF16), ffn1_w_out[0].astype(BF16)

    xt = x.reshape(batch * seq, d)
    for l in range(depth):
        xt, (mix_wi, mix_wo, wi, wo) = _ffn(
            xt, ffn1_g, wi, wo, l, [(w_mix_in, l), (w_mix_out, l), (ffn2_w_in, l), (ffn2_w_out, l)])
        outa, q, k, v = _mix_in(xt, mix_g, mix_wi, sgu_g, sgu_w, sgu_b3, q_g, k_g, l)
        xt = _mix_out(xt, outa, q, k, v, sink, bias_tab, mix_wo, l, batch)
        nxt = [(ffn1_w_in, l + 1), (ffn1_w_out, l + 1)] if l + 1 < depth else []
        xt, nxt_w = _ffn(xt, ffn2_g, wi, wo, l, nxt)
        if nxt:
            wi, wo = nxt_w
    return xt.reshape(batch, seq, d)
```

```python
import functools
import math

import jax
import jax.numpy as jnp
import numpy as np
from jax import lax
from jax.experimental import pallas as pl
from jax.experimental.pallas import tpu as pltpu

F32 = jnp.float32
BF16 = jnp.bfloat16

EPS = 1e-6
NEG_INF = -1e30

LANES = 128
MXU_DIM = 256
VMEM_LIMIT_BYTES = 56 * 1024 * 1024

SGU_GROUPS = 4
CHUNK = 128
N_Q_HEADS = 8
N_KV_HEADS = 2
HEAD_DIM = 64
N_BUCKETS = 32
MAX_DISTANCE = 128
MIX_A = SGU_GROUPS * LANES
MIX_B = N_Q_HEADS * HEAD_DIM
KV_W = N_KV_HEADS * HEAD_DIM
N_VARIANTS = 4

FFN_ROWS = 1024
FFN_CHUNK_TILES = (6, 5)
MIX_IN_ROWS = 1024
MIX_OUT_ROWS = 1024
OUT_PROJ_BLOCKS = 2


def _rms_scale(x, width):
    return lax.rsqrt(jnp.sum(x * x, axis=-1, keepdims=True) * (1.0 / width) + EPS)


def _ffn_kernel(*refs, d_ff, n_casts, layer):
    x_ref, g_ref, win_ref, wout_ref = refs[:4]
    src_refs = refs[4:4 + n_casts]
    o_ref = refs[4 + n_casts]
    dst_refs = refs[5 + n_casts:]
    for src_ref, dst_ref in zip(src_refs, dst_refs):
        dst_ref[...] = src_ref[...].astype(BF16)
    x = x_ref[...]
    h = ((x * _rms_scale(x, x.shape[-1])) * g_ref[layer:layer + 1, :]).astype(BF16)
    assert sum(FFN_CHUNK_TILES) * MXU_DIM == d_ff
    y = None
    lo = 0
    for tiles in FFN_CHUNK_TILES:
        tf = tiles * MXU_DIM
        gate = jnp.dot(h, win_ref[:, lo:lo + tf], preferred_element_type=F32)
        up = jnp.dot(h, win_ref[:, d_ff + lo:d_ff + lo + tf], preferred_element_type=F32)
        a = (gate * jax.nn.sigmoid(gate) * up).astype(BF16)
        part = jnp.dot(a, wout_ref[lo:lo + tf, :], preferred_element_type=F32)
        y = part if y is None else y + part
        lo += tf
    o_ref[...] = x + 0.5 * y


def _ffn(x, gain, w_in, w_out, layer, casts=()):
    t, d = x.shape
    d_ff = w_out.shape[0]
    steps = t // FFN_ROWS
    resident = pl.Buffered(1)
    cast_in_specs, cast_out_specs, cast_out_shapes = [], [], []
    for w, l in casts:
        _, rows, cols = w.shape
        slab = rows // steps
        assert slab * steps == rows and slab % 16 == 0
        cast_in_specs.append(pl.BlockSpec((None, slab, cols), lambda i, l=l: (l, i, 0)))
        cast_out_specs.append(pl.BlockSpec((slab, cols), lambda i: (i, 0)))
        cast_out_shapes.append(jax.ShapeDtypeStruct((rows, cols), BF16))
    outs = pl.pallas_call(
        functools.partial(_ffn_kernel, d_ff=d_ff, n_casts=len(casts), layer=layer),
        grid=(steps,),
        in_specs=[
            pl.BlockSpec((FFN_ROWS, d), lambda i: (i, 0)),
            pl.BlockSpec(gain.shape, lambda i: (0, 0)),
            pl.BlockSpec((d, 2 * d_ff), lambda i: (0, 0), pipeline_mode=resident),
            pl.BlockSpec((d_ff, d), lambda i: (0, 0), pipeline_mode=resident),
        ] + cast_in_specs,
        out_specs=[pl.BlockSpec((FFN_ROWS, d), lambda i: (i, 0))] + cast_out_specs,
        out_shape=[jax.ShapeDtypeStruct((t, d), F32)] + cast_out_shapes,
        compiler_params=pltpu.CompilerParams(
            dimension_semantics=("parallel",), vmem_limit_bytes=VMEM_LIMIT_BYTES),
        name="ffn",
    )(x, gain, w_in, w_out, *[w for w, _ in casts])
    return outs[0], outs[1:]


def _segment_sumsq(x):
    x2 = x * x
    low = lax.broadcasted_iota(jnp.int32, (x.shape[0], LANES), 1) < HEAD_DIM
    parts = []
    for c in range(0, x.shape[-1], LANES):
        t = x2[:, c:c + LANES]
        s_low = jnp.sum(jnp.where(low, t, 0.0), axis=-1, keepdims=True)
        s_high = jnp.sum(jnp.where(low, 0.0, t), axis=-1, keepdims=True)
        parts.append(jnp.where(low, s_low, s_high))
    return parts[0] if len(parts) == 1 else jnp.concatenate(parts, axis=-1)


def _lane_half_variants(x):
    packed = pltpu.bitcast(x, jnp.uint32)
    swapped = pltpu.bitcast(pltpu.roll(packed, HEAD_DIM, axis=1), x.dtype)
    low = lax.broadcasted_iota(jnp.int32, x.shape, 1) < HEAD_DIM
    zero = jnp.zeros_like(x)
    return [
        jnp.where(low, x, zero),
        jnp.where(low, zero, swapped),
        jnp.where(low, swapped, zero),
        jnp.where(low, zero, x),
    ]


def _mix_in_kernel(x_ref, g_ref, win_ref, sgu_g_ref, sgu_w_ref, sgu_b_ref, qg_ref, kg_ref,
                   outa_ref, q_ref, k_ref, v_ref, *, layer):
    gain_row = lambda ref: ref[layer:layer + 1, :]
    x = x_ref[...]
    rows = x.shape[0]
    h = ((x * _rms_scale(x, x.shape[-1])) * gain_row(g_ref)).astype(BF16)
    p = jnp.dot(h, win_ref[...], preferred_element_type=F32)

    z = jax.nn.gelu(p[:, :2 * MIX_A])
    for g in range(SGU_GROUPS):
        u = z[:, g * LANES:(g + 1) * LANES]
        v = z[:, MIX_A + g * LANES:MIX_A + (g + 1) * LANES]
        vn = ((v * _rms_scale(v, LANES))
              * sgu_g_ref[layer:layer + 1, g * LANES:(g + 1) * LANES]).astype(BF16)
        w_s = sgu_w_ref[g].astype(BF16)
        b_s = sgu_b_ref[g]
        chunks = [slice(c * CHUNK, (c + 1) * CHUNK) for c in range(rows // CHUNK)]
        s = jnp.dot(w_s, jnp.concatenate([vn[r] for r in chunks], axis=1),
                    preferred_element_type=F32)
        for c, r in enumerate(chunks):
            outa_ref[r, g * LANES:(g + 1) * LANES] = (
                u[r] * (s[:, c * LANES:(c + 1) * LANES] + b_s)).astype(BF16)

    q = p[:, 2 * MIX_A:2 * MIX_A + MIX_B]
    q_inv = lax.rsqrt(_segment_sumsq(q) * (1.0 / HEAD_DIM) + EPS)
    q_ref[...] = (((q * q_inv) * gain_row(qg_ref)) * (HEAD_DIM ** -0.5)).astype(BF16)

    k = p[:, 2 * MIX_A + MIX_B:2 * MIX_A + MIX_B + KV_W]
    k_inv = lax.rsqrt(_segment_sumsq(k) * (1.0 / HEAD_DIM) + EPS)
    k_ref[...] = ((k * k_inv) * gain_row(kg_ref)).astype(BF16)
    v_ref[...] = p[:, 2 * MIX_A + MIX_B + KV_W:].astype(BF16)


def _mix_in(x, gain, w_in, sgu_g, sgu_w, sgu_b, q_g, k_g, layer):
    t, d = x.shape
    in_cols = w_in.shape[-1]
    rows = MIX_IN_ROWS
    row_spec = lambda w: pl.BlockSpec((rows, w), lambda i: (i, 0))
    layer_spec = lambda *shape: pl.BlockSpec((None,) + shape, lambda i: (layer,) + (0,) * len(shape))
    whole = lambda a: pl.BlockSpec(a.shape, lambda i: (0,) * a.ndim)
    return pl.pallas_call(
        functools.partial(_mix_in_kernel, layer=layer),
        grid=(t // rows,),
        in_specs=[
            row_spec(d),
            whole(gain),
            pl.BlockSpec((d, in_cols), lambda i: (0, 0), pipeline_mode=pl.Buffered(1)),
            whole(sgu_g),
            layer_spec(SGU_GROUPS, CHUNK, CHUNK),
            layer_spec(SGU_GROUPS, CHUNK, LANES),
            whole(q_g),
            whole(k_g),
        ],
        out_specs=[row_spec(MIX_A), row_spec(MIX_B), row_spec(KV_W), row_spec(KV_W)],
        out_shape=[
            jax.ShapeDtypeStruct((t, MIX_A), BF16),
            jax.ShapeDtypeStruct((t, MIX_B), BF16),
            jax.ShapeDtypeStruct((t, KV_W), BF16),
            jax.ShapeDtypeStruct((t, KV_W), BF16),
        ],
        compiler_params=pltpu.CompilerParams(
            dimension_semantics=("parallel",), vmem_limit_bytes=VMEM_LIMIT_BYTES),
        name="mix_in",
    )(x, gain, w_in, sgu_g, sgu_w, sgu_b, q_g, k_g)


HALF_W = 3 * CHUNK
KV_COLS = 2 * LANES


def _mix_out_kernel(sink_ref, x_ref, outa_ref, q_ref, k_ref, v_ref, bias_ref, wout_ref,
                    o_ref, kpad_ref, vpad_ref, *, layer, n_blocks):
    j = pl.program_id(1)
    seq = k_ref.shape[0]

    @pl.when(j == 0)
    def _():
        low = lax.broadcasted_iota(jnp.int32, (CHUNK, LANES), 1) < HEAD_DIM
        ones_low = jnp.where(low, 1.0, 0.0).astype(BF16)
        ones_high = jnp.where(low, 0.0, 1.0).astype(BF16)
        kpad_ref[0:CHUNK, :] = jnp.zeros((CHUNK, N_VARIANTS * LANES), BF16)
        kpad_ref[CHUNK + seq:, :] = jnp.zeros((CHUNK, N_VARIANTS * LANES), BF16)
        k_variants = _lane_half_variants(k_ref[...])
        v_variants = _lane_half_variants(v_ref[...])
        for vi in range(N_VARIANTS):
            kpad_ref[CHUNK:CHUNK + seq, vi * LANES:(vi + 1) * LANES] = k_variants[vi]
            c0 = 2 * vi * LANES
            vpad_ref[0:CHUNK, c0:c0 + LANES] = jnp.zeros((CHUNK, LANES), BF16)
            vpad_ref[CHUNK:CHUNK + seq, c0:c0 + LANES] = v_variants[vi]
            vpad_ref[CHUNK + seq:, c0:c0 + LANES] = jnp.zeros((CHUNK, LANES), BF16)
            ones = ones_low if vi % 2 == 0 else ones_high
            for r0 in range(0, seq + 2 * CHUNK, CHUNK):
                vpad_ref[r0:r0 + CHUNK, c0 + LANES:c0 + 2 * LANES] = ones

    blocks_per_step = x_ref.shape[0] // CHUNK
    first_rows = lax.broadcasted_iota(jnp.int32, (2 * CHUNK, 1), 0) < CHUNK
    low_lanes = lax.broadcasted_iota(jnp.int32, (2 * CHUNK, LANES), 1) < HEAD_DIM
    sink_cols = [[jnp.where(first_rows, sink_ref[layer, 4 * kvh + half], sink_ref[layer, 4 * kvh + half + 2])
                  for half in range(2)] for kvh in range(N_KV_HEADS)]

    def scores(i, kvh):
        n = j * blocks_per_step + i
        edge = jnp.where(n == 0, 0, jnp.where(n == n_blocks - 1, 2, 1))
        win = pl.ds(pl.multiple_of(n * CHUNK, CHUNK), HALF_W)
        r = slice(i * CHUNK, (i + 1) * CHUNK)
        cg = 2 * kvh
        lhs = jnp.concatenate([q_ref[r, cg * LANES:(cg + 1) * LANES],
                               q_ref[r, (cg + 1) * LANES:(cg + 2) * LANES]], axis=0)
        keys = jnp.concatenate([kpad_ref[win, kvh * KV_COLS:kvh * KV_COLS + LANES],
                                kpad_ref[win, kvh * KV_COLS + LANES:(kvh + 1) * KV_COLS]], axis=0)
        s = lax.dot_general(lhs, keys, (((1,), (1,)), ((), ())), preferred_element_type=F32)
        s = s + bias_ref[edge, kvh]
        m = [jnp.maximum(jnp.max(s[:, h * HALF_W:(h + 1) * HALF_W], axis=-1, keepdims=True),
                         sink_cols[kvh][h]) for h in range(2)]
        return s, m, win

    def outputs(kvh, s, m, win):
        e = jnp.concatenate([jnp.exp(s[:, h * HALF_W:(h + 1) * HALF_W] - m[h]) for h in range(2)],
                            axis=-1).astype(BF16)
        vals = jnp.concatenate([vpad_ref[win, 2 * kvh * KV_COLS:(2 * kvh + 1) * KV_COLS],
                                vpad_ref[win, (2 * kvh + 1) * KV_COLS:(2 * kvh + 2) * KV_COLS]], axis=0)
        pv = jnp.dot(e, vals, preferred_element_type=F32)
        e_sink = jnp.where(low_lanes, jnp.exp(sink_cols[kvh][0] - m[0]),
                           jnp.exp(sink_cols[kvh][1] - m[1]))
        o_kv = pv[:, :LANES] * (1.0 / (pv[:, LANES:] + e_sink))
        return [o_kv[0:CHUNK].astype(BF16), o_kv[CHUNK:].astype(BF16)]

    work = [(i, kvh) for i in range(blocks_per_step) for kvh in range(N_KV_HEADS)]
    per_block = 2 * N_KV_HEADS
    tiles = []
    proj_pieces = []

    def proj_piece(rows, mixed, c):
        cols = slice(c * MXU_DIM, (c + 1) * MXU_DIM)
        o_ref[rows, cols] = x_ref[rows, cols] + jnp.dot(mixed, wout_ref[:, cols],
                                                        preferred_element_type=F32)

    pending = scores(*work[0])
    for idx, (i, kvh) in enumerate(work):
        nxt = scores(*work[idx + 1]) if idx + 1 < len(work) else None
        tiles += outputs(kvh, *pending)
        pending = nxt
        if proj_pieces:
            proj_piece(*proj_pieces.pop(0))
        if kvh == N_KV_HEADS - 1 and (i + 1) % OUT_PROJ_BLOCKS == 0:
            rows = slice((i + 1 - OUT_PROJ_BLOCKS) * CHUNK, (i + 1) * CHUNK)
            attn = jnp.concatenate(
                [jnp.concatenate(tiles[b * per_block:(b + 1) * per_block], axis=-1)
                 for b in range(OUT_PROJ_BLOCKS)], axis=0)
            tiles = []
            mixed = jnp.concatenate([outa_ref[rows, :], attn], axis=-1)
            proj_pieces += [(rows, mixed, c) for c in range(o_ref.shape[1] // MXU_DIM)]
    for piece in proj_pieces:
        proj_piece(*piece)


def _mix_out(x, outa, q, k, v, sink, bias_tab, w_out, layer, batch):
    t, d = x.shape
    seq = t // batch
    rows = MIX_OUT_ROWS
    steps = seq // rows
    resident = pl.Buffered(1)
    row_spec = lambda w: pl.BlockSpec((rows, w), lambda b, j: (b * steps + j, 0))
    seq_spec = pl.BlockSpec((seq, KV_W), lambda b, j: (b, 0))
    return pl.pallas_call(
        functools.partial(_mix_out_kernel, layer=layer, n_blocks=seq // CHUNK),
        grid=(batch, steps),
        in_specs=[
            pl.BlockSpec(memory_space=pltpu.SMEM),
            row_spec(d), row_spec(MIX_A), row_spec(MIX_B), seq_spec, seq_spec,
            pl.BlockSpec(bias_tab.shape, lambda b, j: (0, 0, 0, 0), pipeline_mode=resident),
            pl.BlockSpec((MIX_A + MIX_B, d), lambda b, j: (0, 0), pipeline_mode=resident),
        ],
        out_specs=row_spec(d),
        out_shape=jax.ShapeDtypeStruct((t, d), F32),
        scratch_shapes=[
            pltpu.VMEM((seq + 2 * CHUNK, N_VARIANTS * LANES), BF16),
            pltpu.VMEM((seq + 2 * CHUNK, 2 * N_VARIANTS * LANES), BF16),
        ],
        compiler_params=pltpu.CompilerParams(
            dimension_semantics=("parallel", "arbitrary"), vmem_limit_bytes=VMEM_LIMIT_BYTES),
        name="mix_out",
    )(sink, x, outa, q, k, v, bias_tab, w_out)


def _t5_bucket(rel):
    nb = N_BUCKETS // 2
    ret = (rel > 0).astype(np.int32) * nb
    n = np.abs(rel)
    max_exact = nb // 2
    large = max_exact + (np.log(np.maximum(n, 1).astype(np.float32) / max_exact)
                         / math.log(MAX_DISTANCE / max_exact) * (nb - max_exact)).astype(np.int32)
    large = np.minimum(large, nb - 1)
    return ret + np.where(n < max_exact, n, large).astype(np.int32)


def _bias_table(rel_bias):
    rel = np.arange(3 * CHUNK)[None, :] - CHUNK - np.arange(CHUNK)[:, None]
    bucket = jnp.asarray(_t5_bucket(rel), jnp.int32)
    rb = rel_bias.astype(F32)
    bias = jnp.zeros((N_Q_HEADS,) + rel.shape, F32)
    for b in range(N_BUCKETS):
        bias = jnp.where((bucket == b)[None], rb[b][:, None, None], bias)
    band = np.abs(rel) <= CHUNK
    col = np.arange(3 * CHUNK)[None, :]
    edge_masks = [band & (col >= CHUNK), band, band & (col < 2 * CHUNK)]
    allowed = jnp.asarray(sum(m.astype(np.int32) << e for e, m in enumerate(edge_masks)), jnp.int32)
    allowed = jnp.concatenate([allowed, allowed], axis=0)
    head_rows = np.array([[4 * (vi // 2) + (vi % 2), 4 * (vi // 2) + (vi % 2) + 2]
                          for vi in range(N_VARIANTS)])
    per_variant = jnp.stack([jnp.concatenate([bias[a], bias[b]], axis=0) for a, b in head_rows])
    tabs = [jnp.where(((allowed >> e) & 1)[None] == 1, per_variant, NEG_INF)
            for e in range(len(edge_masks))]
    tab = jnp.stack(tabs, axis=0)
    return jnp.concatenate([tab[:, 0::2], tab[:, 1::2]], axis=-1)


def kernel(x, ffn1_norm, ffn1_w_in, ffn1_w_out, mix_norm, w_mix_in, sgu_norm, sgu_w, sgu_b,
           q_norm, k_norm, sink, rel_bias, w_mix_out, ffn2_norm, ffn2_w_in, ffn2_w_out):
    batch, seq, d = x.shape
    depth = ffn1_norm.shape[0]
    assert seq % MIX_OUT_ROWS == 0 and (batch * seq) % FFN_ROWS == 0

    f32 = lambda a: a.astype(F32)
    ffn1_g, mix_g, ffn2_g, sgu_g = f32(ffn1_norm), f32(mix_norm), f32(ffn2_norm), f32(sgu_norm)
    q_g = jnp.tile(f32(q_norm), (1, N_Q_HEADS))
    k_g = jnp.tile(f32(k_norm), (1, N_KV_HEADS))
    sgu_b3 = jnp.broadcast_to(sgu_b.astype(F32)[..., None], sgu_b.shape + (LANES,))
    bias_tab = _bias_table(rel_bias)
    sink = sink.astype(F32)
    wi, wo = ffn1_w_in[0].astype(BF16), ffn1_w_out[0].astype(BF16)

    xt = x.reshape(batch * seq, d)
    for l in range(depth):
        xt, (mix_wi, mix_wo, wi, wo) = _ffn(
            xt, ffn1_g, wi, wo, l, [(w_mix_in, l), (w_mix_out, l), (ffn2_w_in, l), (ffn2_w_out, l)])
        outa, q, k, v = _mix_in(xt, mix_g, mix_wi, sgu_g, sgu_w, sgu_b3, q_g, k_g, l)
        xt = _mix_out(xt, outa, q, k, v, sink, bias_tab, mix_wo, l, batch)
        nxt = [(ffn1_w_in, l + 1), (ffn1_w_out, l + 1)] if l + 1 < depth else []
        xt, nxt_w = _ffn(xt, ffn2_g, wi, wo, l, nxt)
        if nxt:
            wi, wo = nxt_w
    return xt.reshape(batch, seq, d)
```

```python
import functools
import math

import jax
import jax.numpy as jnp
import numpy as np
from jax import lax
from jax.experimental import pallas as pl
from jax.experimental.pallas import tpu as pltpu

F32 = jnp.float32
BF16 = jnp.bfloat16

EPS = 1e-6
NEG_INF = -1e30

LANES = 128
MXU_DIM = 256
VMEM_LIMIT_BYTES = 56 * 1024 * 1024

SGU_GROUPS = 4
CHUNK = 128
N_Q_HEADS = 8
N_KV_HEADS = 2
HEAD_DIM = 64
N_BUCKETS = 32
MAX_DISTANCE = 128
MIX_A = SGU_GROUPS * LANES
MIX_B = N_Q_HEADS * HEAD_DIM
KV_W = N_KV_HEADS * HEAD_DIM
N_VARIANTS = 4

FFN_ROWS = 1024
FFN_CHUNK_TILES = (6, 5)
MIX_IN_ROWS = 1024
MIX_OUT_ROWS = 1024
OUT_PROJ_GROUPS = (2, 2, 2, 2)


def _rms_normed(x, gain):
    w = x.shape[-1]
    inv = lax.rsqrt(jnp.sum(x * x, axis=-1, keepdims=True) + w * EPS)
    return (x * inv) * (gain * math.sqrt(w))


def _ffn_kernel(*refs, d_ff, n_casts, layer):
    x_ref, g_ref, win_ref, wout_ref = refs[:4]
    src_refs = refs[4:4 + n_casts]
    o_ref = refs[4 + n_casts]
    dst_refs = refs[5 + n_casts:]
    for src_ref, dst_ref in zip(src_refs, dst_refs):
        dst_ref[...] = src_ref[...].astype(BF16)
    x = x_ref[...]
    h = _rms_normed(x, g_ref[layer:layer + 1, :]).astype(BF16)
    assert sum(FFN_CHUNK_TILES) * MXU_DIM == d_ff
    y = None
    lo = 0
    for tiles in FFN_CHUNK_TILES:
        tf = tiles * MXU_DIM
        gate = jnp.dot(h, win_ref[:, lo:lo + tf], preferred_element_type=F32)
        up = jnp.dot(h, win_ref[:, d_ff + lo:d_ff + lo + tf], preferred_element_type=F32)
        a = (gate * jax.nn.sigmoid(gate) * up).astype(BF16)
        part = jnp.dot(a, wout_ref[lo:lo + tf, :], preferred_element_type=F32)
        y = part if y is None else y + part
        lo += tf
    o_ref[...] = x + 0.5 * y


def _ffn(x, gain, w_in, w_out, layer, casts=()):
    t, d = x.shape
    d_ff = w_out.shape[0]
    steps = t // FFN_ROWS
    resident = pl.Buffered(1)
    cast_in_specs, cast_out_specs, cast_out_shapes = [], [], []
    for w, l in casts:
        _, rows, cols = w.shape
        slab = rows // steps
        assert slab * steps == rows and slab % 16 == 0
        cast_in_specs.append(pl.BlockSpec((None, slab, cols), lambda i, l=l: (l, i, 0)))
        cast_out_specs.append(pl.BlockSpec((slab, cols), lambda i: (i, 0)))
        cast_out_shapes.append(jax.ShapeDtypeStruct((rows, cols), BF16))
    outs = pl.pallas_call(
        functools.partial(_ffn_kernel, d_ff=d_ff, n_casts=len(casts), layer=layer),
        grid=(steps,),
        in_specs=[
            pl.BlockSpec((FFN_ROWS, d), lambda i: (i, 0)),
            pl.BlockSpec(gain.shape, lambda i: (0, 0)),
            pl.BlockSpec((d, 2 * d_ff), lambda i: (0, 0), pipeline_mode=resident),
            pl.BlockSpec((d_ff, d), lambda i: (0, 0), pipeline_mode=resident),
        ] + cast_in_specs,
        out_specs=[pl.BlockSpec((FFN_ROWS, d), lambda i: (i, 0))] + cast_out_specs,
        out_shape=[jax.ShapeDtypeStruct((t, d), F32)] + cast_out_shapes,
        compiler_params=pltpu.CompilerParams(
            dimension_semantics=("parallel",), vmem_limit_bytes=VMEM_LIMIT_BYTES),
        name="ffn",
    )(x, gain, w_in, w_out, *[w for w, _ in casts])
    return outs[0], outs[1:]


def _gelu_tanh(x):
    k0 = -2.0 * math.sqrt(2.0 / math.pi) * math.log2(math.e)
    k1 = k0 * 0.044715
    return x / (1.0 + jnp.exp2(x * (k0 + k1 * (x * x))))


def _segment_sumsq(x):
    x2 = x * x
    low = lax.broadcasted_iota(jnp.int32, (x.shape[0], LANES), 1) < HEAD_DIM
    parts = []
    for c in range(0, x.shape[-1], LANES):
        t = x2[:, c:c + LANES]
        s_low = jnp.sum(jnp.where(low, t, 0.0), axis=-1, keepdims=True)
        s_high = jnp.sum(jnp.where(low, 0.0, t), axis=-1, keepdims=True)
        parts.append(jnp.where(low, s_low, s_high))
    return parts[0] if len(parts) == 1 else jnp.concatenate(parts, axis=-1)


def _lane_half_variants(x):
    packed = pltpu.bitcast(x, jnp.uint32)
    swapped = pltpu.bitcast(pltpu.roll(packed, HEAD_DIM, axis=1), x.dtype)
    low = lax.broadcasted_iota(jnp.int32, x.shape, 1) < HEAD_DIM
    zero = jnp.zeros_like(x)
    return [
        jnp.where(low, x, zero),
        jnp.where(low, zero, swapped),
        jnp.where(low, swapped, zero),
        jnp.where(low, zero, x),
    ]


def _mix_in_kernel(x_ref, g_ref, win_ref, sgu_g_ref, sgu_w_ref, sgu_b_ref, qg_ref, kg_ref,
                   outa_ref, q_ref, k_ref, v_ref, *, layer):
    gain_row = lambda ref: ref[layer:layer + 1, :]
    x = x_ref[...]
    rows = x.shape[0]
    h = _rms_normed(x, gain_row(g_ref)).astype(BF16)
    p = jnp.dot(h, win_ref[...], preferred_element_type=F32)

    z = _gelu_tanh(p[:, :2 * MIX_A])
    for g in range(SGU_GROUPS):
        u = z[:, g * LANES:(g + 1) * LANES]
        v = z[:, MIX_A + g * LANES:MIX_A + (g + 1) * LANES]
        vn = _rms_normed(v, sgu_g_ref[layer:layer + 1, g * LANES:(g + 1) * LANES]).astype(BF16)
        w_s = sgu_w_ref[g].astype(BF16)
        b_s = sgu_b_ref[g]
        chunks = [slice(c * CHUNK, (c + 1) * CHUNK) for c in range(rows // CHUNK)]
        s = jnp.dot(w_s, jnp.concatenate([vn[r] for r in chunks], axis=1),
                    preferred_element_type=F32)
        for c, r in enumerate(chunks):
            outa_ref[r, g * LANES:(g + 1) * LANES] = (
                u[r] * (s[:, c * LANES:(c + 1) * LANES] + b_s)).astype(BF16)

    q = p[:, 2 * MIX_A:2 * MIX_A + MIX_B]
    q_inv = lax.rsqrt(_segment_sumsq(q) + HEAD_DIM * EPS)
    q_ref[...] = ((q * q_inv) * gain_row(qg_ref)).astype(BF16)

    k = p[:, 2 * MIX_A + MIX_B:2 * MIX_A + MIX_B + KV_W]
    k_inv = lax.rsqrt(_segment_sumsq(k) + HEAD_DIM * EPS)
    k_ref[...] = ((k * k_inv) * (gain_row(kg_ref) * math.sqrt(HEAD_DIM))).astype(BF16)
    v_ref[...] = p[:, 2 * MIX_A + MIX_B + KV_W:].astype(BF16)


def _mix_in(x, gain, w_in, sgu_g, sgu_w, sgu_b, q_g, k_g, layer):
    t, d = x.shape
    in_cols = w_in.shape[-1]
    rows = MIX_IN_ROWS
    row_spec = lambda w: pl.BlockSpec((rows, w), lambda i: (i, 0))
    layer_spec = lambda *shape: pl.BlockSpec((None,) + shape, lambda i: (layer,) + (0,) * len(shape))
    whole = lambda a: pl.BlockSpec(a.shape, lambda i: (0,) * a.ndim)
    return pl.pallas_call(
        functools.partial(_mix_in_kernel, layer=layer),
        grid=(t // rows,),
        in_specs=[
            row_spec(d),
            whole(gain),
            pl.BlockSpec((d, in_cols), lambda i: (0, 0), pipeline_mode=pl.Buffered(1)),
            whole(sgu_g),
            layer_spec(SGU_GROUPS, CHUNK, CHUNK),
            layer_spec(SGU_GROUPS, CHUNK, LANES),
            whole(q_g),
            whole(k_g),
        ],
        out_specs=[row_spec(MIX_A), row_spec(MIX_B), row_spec(KV_W), row_spec(KV_W)],
        out_shape=[
            jax.ShapeDtypeStruct((t, MIX_A), BF16),
            jax.ShapeDtypeStruct((t, MIX_B), BF16),
            jax.ShapeDtypeStruct((t, KV_W), BF16),
            jax.ShapeDtypeStruct((t, KV_W), BF16),
        ],
        compiler_params=pltpu.CompilerParams(
            dimension_semantics=("parallel",), vmem_limit_bytes=VMEM_LIMIT_BYTES),
        name="mix_in",
    )(x, gain, w_in, sgu_g, sgu_w, sgu_b, q_g, k_g)


HALF_W = 3 * CHUNK
KV_COLS = 2 * LANES


def _mix_out_kernel(sink_ref, x_ref, outa_ref, q_ref, k_ref, v_ref, bias_ref, wout_ref,
                    o_ref, kpad_ref, vpad_ref, *, layer, n_blocks):
    j = pl.program_id(1)
    seq = k_ref.shape[0]

    @pl.when(j == 0)
    def _():
        low = lax.broadcasted_iota(jnp.int32, (CHUNK, LANES), 1) < HEAD_DIM
        ones_low = jnp.where(low, 1.0, 0.0).astype(BF16)
        ones_high = jnp.where(low, 0.0, 1.0).astype(BF16)
        kpad_ref[0:CHUNK, :] = jnp.zeros((CHUNK, N_VARIANTS * LANES), BF16)
        kpad_ref[CHUNK + seq:, :] = jnp.zeros((CHUNK, N_VARIANTS * LANES), BF16)
        k_variants = _lane_half_variants(k_ref[...])
        v_variants = _lane_half_variants(v_ref[...])
        for vi in range(N_VARIANTS):
            kpad_ref[CHUNK:CHUNK + seq, vi * LANES:(vi + 1) * LANES] = k_variants[vi]
            c0 = 2 * vi * LANES
            vpad_ref[0:CHUNK, c0:c0 + LANES] = jnp.zeros((CHUNK, LANES), BF16)
            vpad_ref[CHUNK:CHUNK + seq, c0:c0 + LANES] = v_variants[vi]
            vpad_ref[CHUNK + seq:, c0:c0 + LANES] = jnp.zeros((CHUNK, LANES), BF16)
            ones = ones_low if vi % 2 == 0 else ones_high
            for r0 in range(0, seq + 2 * CHUNK, CHUNK):
                vpad_ref[r0:r0 + CHUNK, c0 + LANES:c0 + 2 * LANES] = ones

    blocks_per_step = x_ref.shape[0] // CHUNK
    first_rows = lax.broadcasted_iota(jnp.int32, (2 * CHUNK, 1), 0) < CHUNK
    low_lanes = lax.broadcasted_iota(jnp.int32, (2 * CHUNK, LANES), 1) < HEAD_DIM
    sink_cols = [[jnp.where(first_rows, sink_ref[layer, 4 * kvh + half], sink_ref[layer, 4 * kvh + half + 2])
                  for half in range(2)] for kvh in range(N_KV_HEADS)]

    def scores(i, kvh):
        n = j * blocks_per_step + i
        edge = jnp.where(n == 0, 0, jnp.where(n == n_blocks - 1, 2, 1))
        win = pl.ds(pl.multiple_of(n * CHUNK, CHUNK), HALF_W)
        r = slice(i * CHUNK, (i + 1) * CHUNK)
        cg = 2 * kvh
        lhs = jnp.concatenate([q_ref[r, cg * LANES:(cg + 1) * LANES],
                               q_ref[r, (cg + 1) * LANES:(cg + 2) * LANES]], axis=0)
        keys = jnp.concatenate([kpad_ref[win, kvh * KV_COLS:kvh * KV_COLS + LANES],
                                kpad_ref[win, kvh * KV_COLS + LANES:(kvh + 1) * KV_COLS]], axis=0)
        s = lax.dot_general(lhs, keys, (((1,), (1,)), ((), ())), preferred_element_type=F32)
        s = s + bias_ref[edge, kvh]
        m = [jnp.maximum(jnp.max(s[:, h * HALF_W:(h + 1) * HALF_W], axis=-1, keepdims=True),
                         sink_cols[kvh][h]) for h in range(2)]
        return s, m, win

    def outputs(kvh, s, m, win):
        e = jnp.concatenate([jnp.exp(s[:, h * HALF_W:(h + 1) * HALF_W] - m[h]) for h in range(2)],
                            axis=-1).astype(BF16)
        vals = jnp.concatenate([vpad_ref[win, 2 * kvh * KV_COLS:(2 * kvh + 1) * KV_COLS],
                                vpad_ref[win, (2 * kvh + 1) * KV_COLS:(2 * kvh + 2) * KV_COLS]], axis=0)
        pv = jnp.dot(e, vals, preferred_element_type=F32)
        e_sink = jnp.where(low_lanes, jnp.exp(sink_cols[kvh][0] - m[0]),
                           jnp.exp(sink_cols[kvh][1] - m[1]))
        o_kv = pv[:, :LANES] * (1.0 / (pv[:, LANES:] + e_sink))
        return [o_kv[0:CHUNK].astype(BF16), o_kv[CHUNK:].astype(BF16)]

    work = [(i, kvh) for i in range(blocks_per_step) for kvh in range(N_KV_HEADS)]
    per_block = 2 * N_KV_HEADS
    assert sum(OUT_PROJ_GROUPS) == blocks_per_step
    group_ends = set(np.cumsum(OUT_PROJ_GROUPS).tolist())
    tiles = []
    proj_pieces = []

    def proj_piece(rows, mixed, c):
        cols = slice(c * MXU_DIM, (c + 1) * MXU_DIM)
        o_ref[rows, cols] = x_ref[rows, cols] + jnp.dot(mixed, wout_ref[:, cols],
                                                        preferred_element_type=F32)

    pending = scores(*work[0])
    for idx, (i, kvh) in enumerate(work):
        nxt = scores(*work[idx + 1]) if idx + 1 < len(work) else None
        tiles += outputs(kvh, *pending)
        pending = nxt
        if proj_pieces:
            proj_piece(*proj_pieces.pop(0))
        if kvh == N_KV_HEADS - 1 and (i + 1) in group_ends:
            n_group = len(tiles) // per_block
            rows = slice((i + 1 - n_group) * CHUNK, (i + 1) * CHUNK)
            attn = jnp.concatenate(
                [jnp.concatenate(tiles[b * per_block:(b + 1) * per_block], axis=-1)
                 for b in range(n_group)], axis=0)
            tiles = []
            mixed = jnp.concatenate([outa_ref[rows, :], attn], axis=-1)
            proj_pieces += [(rows, mixed, c) for c in range(o_ref.shape[1] // MXU_DIM)]
    for piece in proj_pieces:
        proj_piece(*piece)


def _mix_out(x, outa, q, k, v, sink, bias_tab, w_out, layer, batch):
    t, d = x.shape
    seq = t // batch
    rows = MIX_OUT_ROWS
    steps = seq // rows
    resident = pl.Buffered(1)
    row_spec = lambda w: pl.BlockSpec((rows, w), lambda b, j: (b * steps + j, 0))
    seq_spec = pl.BlockSpec((seq, KV_W), lambda b, j: (b, 0))
    return pl.pallas_call(
        functools.partial(_mix_out_kernel, layer=layer, n_blocks=seq // CHUNK),
        grid=(batch, steps),
        in_specs=[
            pl.BlockSpec(memory_space=pltpu.SMEM),
            row_spec(d), row_spec(MIX_A), row_spec(MIX_B), seq_spec, seq_spec,
            pl.BlockSpec(bias_tab.shape, lambda b, j: (0, 0, 0, 0), pipeline_mode=resident),
            pl.BlockSpec((MIX_A + MIX_B, d), lambda b, j: (0, 0), pipeline_mode=resident),
        ],
        out_specs=row_spec(d),
        out_shape=jax.ShapeDtypeStruct((t, d), F32),
        scratch_shapes=[
            pltpu.VMEM((seq + 2 * CHUNK, N_VARIANTS * LANES), BF16),
            pltpu.VMEM((seq + 2 * CHUNK, 2 * N_VARIANTS * LANES), BF16),
        ],
        compiler_params=pltpu.CompilerParams(
            dimension_semantics=("parallel", "arbitrary"), vmem_limit_bytes=VMEM_LIMIT_BYTES),
        name="mix_out",
    )(sink, x, outa, q, k, v, bias_tab, w_out)


def _t5_bucket(rel):
    nb = N_BUCKETS // 2
    ret = (rel > 0).astype(np.int32) * nb
    n = np.abs(rel)
    max_exact = nb // 2
    large = max_exact + (np.log(np.maximum(n, 1).astype(np.float32) / max_exact)
                         / math.log(MAX_DISTANCE / max_exact) * (nb - max_exact)).astype(np.int32)
    large = np.minimum(large, nb - 1)
    return ret + np.where(n < max_exact, n, large).astype(np.int32)


def _bias_table(rel_bias):
    rel = np.arange(3 * CHUNK)[None, :] - CHUNK - np.arange(CHUNK)[:, None]
    bucket = jnp.asarray(_t5_bucket(rel), jnp.int32)
    rb = rel_bias.astype(F32)
    bias = jnp.zeros((N_Q_HEADS,) + rel.shape, F32)
    for b in range(N_BUCKETS):
        bias = jnp.where((bucket == b)[None], rb[b][:, None, None], bias)
    band = np.abs(rel) <= CHUNK
    col = np.arange(3 * CHUNK)[None, :]
    edge_masks = [band & (col >= CHUNK), band, band & (col < 2 * CHUNK)]
    allowed = jnp.asarray(sum(m.astype(np.int32) << e for e, m in enumerate(edge_masks)), jnp.int32)
    allowed = jnp.concatenate([allowed, allowed], axis=0)
    head_rows = np.array([[4 * (vi // 2) + (vi % 2), 4 * (vi // 2) + (vi % 2) + 2]
                          for vi in range(N_VARIANTS)])
    per_variant = jnp.stack([jnp.concatenate([bias[a], bias[b]], axis=0) for a, b in head_rows])
    tabs = [jnp.where(((allowed >> e) & 1)[None] == 1, per_variant, NEG_INF)
            for e in range(len(edge_masks))]
    tab = jnp.stack(tabs, axis=0)
    return jnp.concatenate([tab[:, 0::2], tab[:, 1::2]], axis=-1)


def kernel(x, ffn1_norm, ffn1_w_in, ffn1_w_out, mix_norm, w_mix_in, sgu_norm, sgu_w, sgu_b,
           q_norm, k_norm, sink, rel_bias, w_mix_out, ffn2_norm, ffn2_w_in, ffn2_w_out):
    batch, seq, d = x.shape
    depth = ffn1_norm.shape[0]
    assert seq % MIX_OUT_ROWS == 0 and (batch * seq) % FFN_ROWS == 0

    f32 = lambda a: a.astype(F32)
    ffn1_g, mix_g, ffn2_g, sgu_g = f32(ffn1_norm), f32(mix_norm), f32(ffn2_norm), f32(sgu_norm)
    q_g = jnp.tile(f32(q_norm), (1, N_Q_HEADS))
    k_g = jnp.tile(f32(k_norm), (1, N_KV_HEADS))
    sgu_b3 = jnp.broadcast_to(sgu_b.astype(F32)[..., None], sgu_b.shape + (LANES,))
    bias_tab = _bias_table(rel_bias)
    sink = sink.astype(F32)
    wi, wo = ffn1_w_in[0].astype(BF16), ffn1_w_out[0].astype(BF16)

    xt = x.reshape(batch * seq, d)
    for l in range(depth):
        xt, (mix_wi, mix_wo, wi, wo) = _ffn(
            xt, ffn1_g, wi, wo, l, [(w_mix_in, l), (w_mix_out, l), (ffn2_w_in, l), (ffn2_w_out, l)])
        outa, q, k, v = _mix_in(xt, mix_g, mix_wi, sgu_g, sgu_w, sgu_b3, q_g, k_g, l)
        xt = _mix_out(xt, outa, q, k, v, sink, bias_tab, mix_wo, l, batch)
        nxt = [(ffn1_w_in, l + 1), (ffn1_w_out, l + 1)] if l + 1 < depth else []
        xt, nxt_w = _ffn(xt, ffn2_g, wi, wo, l, nxt)
        if nxt:
            wi, wo = nxt_w
    return xt.reshape(batch, seq, d)
```

```python
import functools
import math

import jax
import jax.numpy as jnp
import numpy as np
from jax import lax
from jax.experimental import pallas as pl
from jax.experimental.pallas import tpu as pltpu

F32 = jnp.float32
BF16 = jnp.bfloat16

EPS = 1e-6
NEG_INF = -1e30

LANES = 128
MXU_DIM = 256
VMEM_LIMIT_BYTES = 56 * 1024 * 1024

SGU_GROUPS = 4
CHUNK = 128
N_Q_HEADS = 8
N_KV_HEADS = 2
HEAD_DIM = 64
N_BUCKETS = 32
MAX_DISTANCE = 128
MIX_A = SGU_GROUPS * LANES
MIX_B = N_Q_HEADS * HEAD_DIM
KV_W = N_KV_HEADS * HEAD_DIM
N_VARIANTS = 4

FFN_ROWS = 1024
FFN_CHUNK_TILES = (6, 5)
MIX_IN_ROWS = 1024
MIX_OUT_ROWS = 1024
OUT_PROJ_GROUPS = (2, 2, 2, 2)


def _rms_normed(x, gain):
    w = x.shape[-1]
    inv = lax.rsqrt(jnp.sum(x * x, axis=-1, keepdims=True) + w * EPS)
    return (x * inv) * (gain * math.sqrt(w))


def _ffn_kernel(*refs, d_ff, n_casts, layer):
    x_ref, g_ref, win_hbm, wout_hbm = refs[:4]
    src_refs = refs[4:4 + n_casts]
    o_ref = refs[4 + n_casts]
    dst_refs = refs[5 + n_casts:5 + 2 * n_casts]
    win_ref, wout_ref, stage_in_ref, stage_out_ref, sem = refs[5 + 2 * n_casts:]
    assert sum(FFN_CHUNK_TILES) * MXU_DIM == d_ff
    n_tiles = d_ff // MXU_DIM
    step = pl.program_id(0)

    for src_ref, dst_ref in zip(src_refs, dst_refs):
        dst_ref[...] = src_ref[...].astype(BF16)

    def normed_input():
        return _rms_normed(x_ref[...], g_ref[layer:layer + 1, :]).astype(BF16)

    def swiglu_part(h, w_gate, w_up, w_down):
        gate = jnp.dot(h, w_gate, preferred_element_type=F32)
        up = jnp.dot(h, w_up, preferred_element_type=F32)
        a = (gate * jax.nn.sigmoid(gate) * up).astype(BF16)
        return jnp.dot(a, w_down, preferred_element_type=F32)

    def accumulate(first, part):
        if first:
            o_ref[...] = x_ref[...] + 0.5 * part
        else:
            o_ref[...] += 0.5 * part

    def tile_copies(c, slot):
        lo = c * MXU_DIM
        return [
            pltpu.make_async_copy(win_hbm.at[layer, :, pl.ds(lo, MXU_DIM)],
                                  stage_in_ref.at[slot, :, pl.ds(0, MXU_DIM)], sem.at[slot, 0]),
            pltpu.make_async_copy(win_hbm.at[layer, :, pl.ds(d_ff + lo, MXU_DIM)],
                                  stage_in_ref.at[slot, :, pl.ds(MXU_DIM, MXU_DIM)], sem.at[slot, 1]),
            pltpu.make_async_copy(wout_hbm.at[layer, pl.ds(lo, MXU_DIM), :],
                                  stage_out_ref.at[slot], sem.at[slot, 2]),
        ]

    @pl.when(step == 0)
    def _():
        for n, cp in enumerate(tile_copies(0, 0)):
            cp.start(priority=n % 2)
        h = normed_input()
        for c in range(n_tiles):
            slot = c % 2
            if c + 1 < n_tiles:
                for n, cp in enumerate(tile_copies(c + 1, 1 - slot)):
                    cp.start(priority=n % 2)
            for cp in tile_copies(c, slot):
                cp.wait()
            lo = c * MXU_DIM
            w_gate = stage_in_ref[slot, :, 0:MXU_DIM].astype(BF16)
            w_up = stage_in_ref[slot, :, MXU_DIM:].astype(BF16)
            w_down = stage_out_ref[slot].astype(BF16)
            win_ref[:, lo:lo + MXU_DIM] = w_gate
            win_ref[:, d_ff + lo:d_ff + lo + MXU_DIM] = w_up
            wout_ref[lo:lo + MXU_DIM, :] = w_down
            accumulate(c == 0, swiglu_part(h, w_gate, w_up, w_down))

    @pl.when(step > 0)
    def _():
        h = normed_input()
        lo = 0
        for tiles in FFN_CHUNK_TILES:
            tf = tiles * MXU_DIM
            accumulate(lo == 0, swiglu_part(h, win_ref[:, lo:lo + tf],
                                            win_ref[:, d_ff + lo:d_ff + lo + tf],
                                            wout_ref[lo:lo + tf, :]))
            lo += tf


def _ffn(x, gain, w_in, w_out, layer, casts=()):
    t, d = x.shape
    d_ff = w_out.shape[1]
    steps = t // FFN_ROWS
    cast_in_specs, cast_out_specs, cast_out_shapes = [], [], []
    for w, l in casts:
        _, rows, cols = w.shape
        slab = rows // steps
        assert slab * steps == rows and slab % 16 == 0
        cast_in_specs.append(pl.BlockSpec((None, slab, cols), lambda i, l=l: (l, i, 0)))
        cast_out_specs.append(pl.BlockSpec((slab, cols), lambda i: (i, 0)))
        cast_out_shapes.append(jax.ShapeDtypeStruct((rows, cols), BF16))
    outs = pl.pallas_call(
        functools.partial(_ffn_kernel, d_ff=d_ff, n_casts=len(casts), layer=layer),
        grid=(steps,),
        in_specs=[
            pl.BlockSpec((FFN_ROWS, d), lambda i: (i, 0)),
            pl.BlockSpec(gain.shape, lambda i: (0, 0)),
            pl.BlockSpec(memory_space=pl.ANY),
            pl.BlockSpec(memory_space=pl.ANY),
        ] + cast_in_specs,
        out_specs=[pl.BlockSpec((FFN_ROWS, d), lambda i: (i, 0))] + cast_out_specs,
        out_shape=[jax.ShapeDtypeStruct((t, d), F32)] + cast_out_shapes,
        scratch_shapes=[
            pltpu.VMEM((d, 2 * d_ff), BF16),
            pltpu.VMEM((d_ff, d), BF16),
            pltpu.VMEM((2, d, 2 * MXU_DIM), F32),
            pltpu.VMEM((2, MXU_DIM, d), F32),
            pltpu.SemaphoreType.DMA((2, 3)),
        ],
        compiler_params=pltpu.CompilerParams(
            dimension_semantics=("arbitrary",), vmem_limit_bytes=VMEM_LIMIT_BYTES),
        name="ffn",
    )(x, gain, w_in, w_out, *[w for w, _ in casts])
    return outs[0], outs[1:]


def _gelu_tanh(x):
    k0 = -2.0 * math.sqrt(2.0 / math.pi) * math.log2(math.e)
    k1 = k0 * 0.044715
    return x / (1.0 + jnp.exp2(x * (k0 + k1 * (x * x))))


def _segment_sumsq(x):
    x2 = x * x
    low = lax.broadcasted_iota(jnp.int32, (x.shape[0], LANES), 1) < HEAD_DIM
    parts = []
    for c in range(0, x.shape[-1], LANES):
        t = x2[:, c:c + LANES]
        s_low = jnp.sum(jnp.where(low, t, 0.0), axis=-1, keepdims=True)
        s_high = jnp.sum(jnp.where(low, 0.0, t), axis=-1, keepdims=True)
        parts.append(jnp.where(low, s_low, s_high))
    return parts[0] if len(parts) == 1 else jnp.concatenate(parts, axis=-1)


def _lane_half_variants(x):
    packed = pltpu.bitcast(x, jnp.uint32)
    swapped = pltpu.bitcast(pltpu.roll(packed, HEAD_DIM, axis=1), x.dtype)
    low = lax.broadcasted_iota(jnp.int32, x.shape, 1) < HEAD_DIM
    zero = jnp.zeros_like(x)
    return [
        jnp.where(low, x, zero),
        jnp.where(low, zero, swapped),
        jnp.where(low, swapped, zero),
        jnp.where(low, zero, x),
    ]


def _mix_in_kernel(x_ref, g_ref, win_ref, sgu_g_ref, sgu_w_ref, sgu_b_ref, qg_ref, kg_ref,
                   outa_ref, q_ref, k_ref, v_ref, *, layer):
    gain_row = lambda ref: ref[layer:layer + 1, :]
    x = x_ref[...]
    rows = x.shape[0]
    h = _rms_normed(x, gain_row(g_ref)).astype(BF16)
    p = jnp.dot(h, win_ref[...], preferred_element_type=F32)

    z = _gelu_tanh(p[:, :2 * MIX_A])
    for g in range(SGU_GROUPS):
        u = z[:, g * LANES:(g + 1) * LANES]
        v = z[:, MIX_A + g * LANES:MIX_A + (g + 1) * LANES]
        vn = _rms_normed(v, sgu_g_ref[layer:layer + 1, g * LANES:(g + 1) * LANES]).astype(BF16)
        w_s = sgu_w_ref[g].astype(BF16)
        b_s = sgu_b_ref[g]
        chunks = [slice(c * CHUNK, (c + 1) * CHUNK) for c in range(rows // CHUNK)]
        s = jnp.dot(w_s, jnp.concatenate([vn[r] for r in chunks], axis=1),
                    preferred_element_type=F32)
        for c, r in enumerate(chunks):
            outa_ref[r, g * LANES:(g + 1) * LANES] = (
                u[r] * (s[:, c * LANES:(c + 1) * LANES] + b_s)).astype(BF16)

    q = p[:, 2 * MIX_A:2 * MIX_A + MIX_B]
    q_inv = lax.rsqrt(_segment_sumsq(q) + HEAD_DIM * EPS)
    q_ref[...] = ((q * q_inv) * gain_row(qg_ref)).astype(BF16)

    k = p[:, 2 * MIX_A + MIX_B:2 * MIX_A + MIX_B + KV_W]
    k_inv = lax.rsqrt(_segment_sumsq(k) + HEAD_DIM * EPS)
    k_ref[...] = ((k * k_inv) * (gain_row(kg_ref) * math.sqrt(HEAD_DIM))).astype(BF16)
    v_ref[...] = p[:, 2 * MIX_A + MIX_B + KV_W:].astype(BF16)


def _mix_in(x, gain, w_in, sgu_g, sgu_w, sgu_b, q_g, k_g, layer):
    t, d = x.shape
    in_cols = w_in.shape[-1]
    rows = MIX_IN_ROWS
    row_spec = lambda w: pl.BlockSpec((rows, w), lambda i: (i, 0))
    layer_spec = lambda *shape: pl.BlockSpec((None,) + shape, lambda i: (layer,) + (0,) * len(shape))
    whole = lambda a: pl.BlockSpec(a.shape, lambda i: (0,) * a.ndim)
    return pl.pallas_call(
        functools.partial(_mix_in_kernel, layer=layer),
        grid=(t // rows,),
        in_specs=[
            row_spec(d),
            whole(gain),
            pl.BlockSpec((d, in_cols), lambda i: (0, 0), pipeline_mode=pl.Buffered(1)),
            whole(sgu_g),
            layer_spec(SGU_GROUPS, CHUNK, CHUNK),
            layer_spec(SGU_GROUPS, CHUNK, LANES),
            whole(q_g),
            whole(k_g),
        ],
        out_specs=[row_spec(MIX_A), row_spec(MIX_B), row_spec(KV_W), row_spec(KV_W)],
        out_shape=[
            jax.ShapeDtypeStruct((t, MIX_A), BF16),
            jax.ShapeDtypeStruct((t, MIX_B), BF16),
            jax.ShapeDtypeStruct((t, KV_W), BF16),
            jax.ShapeDtypeStruct((t, KV_W), BF16),
        ],
        compiler_params=pltpu.CompilerParams(
            dimension_semantics=("parallel",), vmem_limit_bytes=VMEM_LIMIT_BYTES),
        name="mix_in",
    )(x, gain, w_in, sgu_g, sgu_w, sgu_b, q_g, k_g)


HALF_W = 3 * CHUNK
KV_COLS = 2 * LANES


def _mix_out_kernel(sink_ref, x_ref, outa_ref, q_ref, k_ref, v_ref, bias_ref, wout_ref,
                    o_ref, kpad_ref, vpad_ref, *, layer, n_blocks):
    j = pl.program_id(1)
    seq = k_ref.shape[0]

    @pl.when(j == 0)
    def _():
        low = lax.broadcasted_iota(jnp.int32, (CHUNK, LANES), 1) < HEAD_DIM
        ones_low = jnp.where(low, 1.0, 0.0).astype(BF16)
        ones_high = jnp.where(low, 0.0, 1.0).astype(BF16)
        kpad_ref[0:CHUNK, :] = jnp.zeros((CHUNK, N_VARIANTS * LANES), BF16)
        kpad_ref[CHUNK + seq:, :] = jnp.zeros((CHUNK, N_VARIANTS * LANES), BF16)
        k_variants = _lane_half_variants(k_ref[...])
        v_variants = _lane_half_variants(v_ref[...])
        for vi in range(N_VARIANTS):
            kpad_ref[CHUNK:CHUNK + seq, vi * LANES:(vi + 1) * LANES] = k_variants[vi]
            c0 = 2 * vi * LANES
            vpad_ref[0:CHUNK, c0:c0 + LANES] = jnp.zeros((CHUNK, LANES), BF16)
            vpad_ref[CHUNK:CHUNK + seq, c0:c0 + LANES] = v_variants[vi]
            vpad_ref[CHUNK + seq:, c0:c0 + LANES] = jnp.zeros((CHUNK, LANES), BF16)
            ones = ones_low if vi % 2 == 0 else ones_high
            for r0 in range(0, seq + 2 * CHUNK, CHUNK):
                vpad_ref[r0:r0 + CHUNK, c0 + LANES:c0 + 2 * LANES] = ones

    blocks_per_step = x_ref.shape[0] // CHUNK
    first_rows = lax.broadcasted_iota(jnp.int32, (2 * CHUNK, 1), 0) < CHUNK
    low_lanes = lax.broadcasted_iota(jnp.int32, (2 * CHUNK, LANES), 1) < HEAD_DIM
    sink_cols = [[jnp.where(first_rows, sink_ref[layer, 4 * kvh + half], sink_ref[layer, 4 * kvh + half + 2])
                  for half in range(2)] for kvh in range(N_KV_HEADS)]

    def scores(i, kvh):
        n = j * blocks_per_step + i
        edge = jnp.where(n == 0, 0, jnp.where(n == n_blocks - 1, 2, 1))
        win = pl.ds(pl.multiple_of(n * CHUNK, CHUNK), HALF_W)
        r = slice(i * CHUNK, (i + 1) * CHUNK)
        cg = 2 * kvh
        lhs = jnp.concatenate([q_ref[r, cg * LANES:(cg + 1) * LANES],
                               q_ref[r, (cg + 1) * LANES:(cg + 2) * LANES]], axis=0)
        keys = jnp.concatenate([kpad_ref[win, kvh * KV_COLS:kvh * KV_COLS + LANES],
                                kpad_ref[win, kvh * KV_COLS + LANES:(kvh + 1) * KV_COLS]], axis=0)
        s = lax.dot_general(lhs, keys, (((1,), (1,)), ((), ())), preferred_element_type=F32)
        s = s + bias_ref[edge, kvh]
        m = [jnp.maximum(jnp.max(s[:, h * HALF_W:(h + 1) * HALF_W], axis=-1, keepdims=True),
                         sink_cols[kvh][h]) for h in range(2)]
        return s, m, win

    def outputs(kvh, s, m, win):
        e = jnp.concatenate([jnp.exp(s[:, h * HALF_W:(h + 1) * HALF_W] - m[h]) for h in range(2)],
                            axis=-1).astype(BF16)
        vals = jnp.concatenate([vpad_ref[win, 2 * kvh * KV_COLS:(2 * kvh + 1) * KV_COLS],
                                vpad_ref[win, (2 * kvh + 1) * KV_COLS:(2 * kvh + 2) * KV_COLS]], axis=0)
        pv = jnp.dot(e, vals, preferred_element_type=F32)
        e_sink = jnp.where(low_lanes, jnp.exp(sink_cols[kvh][0] - m[0]),
                           jnp.exp(sink_cols[kvh][1] - m[1]))
        o_kv = pv[:, :LANES] * (1.0 / (pv[:, LANES:] + e_sink))
        return [o_kv[0:CHUNK].astype(BF16), o_kv[CHUNK:].astype(BF16)]

    work = [(i, kvh) for i in range(blocks_per_step) for kvh in range(N_KV_HEADS)]
    per_block = 2 * N_KV_HEADS
    assert sum(OUT_PROJ_GROUPS) == blocks_per_step
    group_ends = set(np.cumsum(OUT_PROJ_GROUPS).tolist())
    tiles = []
    proj_pieces = []

    def proj_piece(rows, mixed, c):
        cols = slice(c * MXU_DIM, (c + 1) * MXU_DIM)
        o_ref[rows, cols] = x_ref[rows, cols] + jnp.dot(mixed, wout_ref[:, cols],
                                                        preferred_element_type=F32)

    pending = scores(*work[0])
    for idx, (i, kvh) in enumerate(work):
        nxt = scores(*work[idx + 1]) if idx + 1 < len(work) else None
        tiles += outputs(kvh, *pending)
        pending = nxt
        if proj_pieces:
            proj_piece(*proj_pieces.pop(0))
        if kvh == N_KV_HEADS - 1 and (i + 1) in group_ends:
            n_group = len(tiles) // per_block
            rows = slice((i + 1 - n_group) * CHUNK, (i + 1) * CHUNK)
            attn = jnp.concatenate(
                [jnp.concatenate(tiles[b * per_block:(b + 1) * per_block], axis=-1)
                 for b in range(n_group)], axis=0)
            tiles = []
            mixed = jnp.concatenate([outa_ref[rows, :], attn], axis=-1)
            proj_pieces += [(rows, mixed, c) for c in range(o_ref.shape[1] // MXU_DIM)]
    for piece in proj_pieces:
        proj_piece(*piece)


def _mix_out(x, outa, q, k, v, sink, bias_tab, w_out, layer, batch):
    t, d = x.shape
    seq = t // batch
    rows = MIX_OUT_ROWS
    steps = seq // rows
    resident = pl.Buffered(1)
    row_spec = lambda w: pl.BlockSpec((rows, w), lambda b, j: (b * steps + j, 0))
    seq_spec = pl.BlockSpec((seq, KV_W), lambda b, j: (b, 0))
    return pl.pallas_call(
        functools.partial(_mix_out_kernel, layer=layer, n_blocks=seq // CHUNK),
        grid=(batch, steps),
        in_specs=[
            pl.BlockSpec(memory_space=pltpu.SMEM),
            row_spec(d), row_spec(MIX_A), row_spec(MIX_B), seq_spec, seq_spec,
            pl.BlockSpec(bias_tab.shape, lambda b, j: (0, 0, 0, 0), pipeline_mode=resident),
            pl.BlockSpec((MIX_A + MIX_B, d), lambda b, j: (0, 0), pipeline_mode=resident),
        ],
        out_specs=row_spec(d),
        out_shape=jax.ShapeDtypeStruct((t, d), F32),
        scratch_shapes=[
            pltpu.VMEM((seq + 2 * CHUNK, N_VARIANTS * LANES), BF16),
            pltpu.VMEM((seq + 2 * CHUNK, 2 * N_VARIANTS * LANES), BF16),
        ],
        compiler_params=pltpu.CompilerParams(
            dimension_semantics=("parallel", "arbitrary"), vmem_limit_bytes=VMEM_LIMIT_BYTES),
        name="mix_out",
    )(sink, x, outa, q, k, v, bias_tab, w_out)


def _t5_bucket(rel):
    nb = N_BUCKETS // 2
    ret = (rel > 0).astype(np.int32) * nb
    n = np.abs(rel)
    max_exact = nb // 2
    large = max_exact + (np.log(np.maximum(n, 1).astype(np.float32) / max_exact)
                         / math.log(MAX_DISTANCE / max_exact) * (nb - max_exact)).astype(np.int32)
    large = np.minimum(large, nb - 1)
    return ret + np.where(n < max_exact, n, large).astype(np.int32)


def _bias_table(rel_bias):
    rel = np.arange(3 * CHUNK)[None, :] - CHUNK - np.arange(CHUNK)[:, None]
    bucket = jnp.asarray(_t5_bucket(rel), jnp.int32)
    rb = rel_bias.astype(F32)
    bias = jnp.zeros((N_Q_HEADS,) + rel.shape, F32)
    for b in range(N_BUCKETS):
        bias = jnp.where((bucket == b)[None], rb[b][:, None, None], bias)
    band = np.abs(rel) <= CHUNK
    col = np.arange(3 * CHUNK)[None, :]
    edge_masks = [band & (col >= CHUNK), band, band & (col < 2 * CHUNK)]
    allowed = jnp.asarray(sum(m.astype(np.int32) << e for e, m in enumerate(edge_masks)), jnp.int32)
    allowed = jnp.concatenate([allowed, allowed], axis=0)
    head_rows = np.array([[4 * (vi // 2) + (vi % 2), 4 * (vi // 2) + (vi % 2) + 2]
                          for vi in range(N_VARIANTS)])
    per_variant = jnp.stack([jnp.concatenate([bias[a], bias[b]], axis=0) for a, b in head_rows])
    tabs = [jnp.where(((allowed >> e) & 1)[None] == 1, per_variant, NEG_INF)
            for e in range(len(edge_masks))]
    tab = jnp.stack(tabs, axis=0)
    return jnp.concatenate([tab[:, 0::2], tab[:, 1::2]], axis=-1)


def kernel(x, ffn1_norm, ffn1_w_in, ffn1_w_out, mix_norm, w_mix_in, sgu_norm, sgu_w, sgu_b,
           q_norm, k_norm, sink, rel_bias, w_mix_out, ffn2_norm, ffn2_w_in, ffn2_w_out):
    batch, seq, d = x.shape
    depth = ffn1_norm.shape[0]
    assert seq % MIX_OUT_ROWS == 0 and (batch * seq) % FFN_ROWS == 0

    f32 = lambda a: a.astype(F32)
    ffn1_g, mix_g, ffn2_g, sgu_g = f32(ffn1_norm), f32(mix_norm), f32(ffn2_norm), f32(sgu_norm)
    q_g = jnp.tile(f32(q_norm), (1, N_Q_HEADS))
    k_g = jnp.tile(f32(k_norm), (1, N_KV_HEADS))
    sgu_b3 = jnp.broadcast_to(sgu_b.astype(F32)[..., None], sgu_b.shape + (LANES,))
    bias_tab = _bias_table(rel_bias)
    sink = sink.astype(F32)
    xt = x.reshape(batch * seq, d)
    for l in range(depth):
        xt, (mix_wi, mix_wo) = _ffn(xt, ffn1_g, ffn1_w_in, ffn1_w_out, l,
                                    [(w_mix_in, l), (w_mix_out, l)])
        outa, q, k, v = _mix_in(xt, mix_g, mix_wi, sgu_g, sgu_w, sgu_b3, q_g, k_g, l)
        xt = _mix_out(xt, outa, q, k, v, sink, bias_tab, mix_wo, l, batch)
        xt, _ = _ffn(xt, ffn2_g, ffn2_w_in, ffn2_w_out, l)
    return xt.reshape(batch, seq, d)
```

```python
import functools
import math

import jax
import jax.numpy as jnp
import numpy as np
from jax import lax
from jax.experimental import pallas as pl
from jax.experimental.pallas import tpu as pltpu

F32 = jnp.float32
BF16 = jnp.bfloat16

EPS = 1e-6
NEG_INF = -1e30

LANES = 128
MXU_DIM = 256
VMEM_LIMIT_BYTES = 56 * 1024 * 1024

SGU_GROUPS = 4
CHUNK = 128
N_Q_HEADS = 8
N_KV_HEADS = 2
HEAD_DIM = 64
N_BUCKETS = 32
MAX_DISTANCE = 128
MIX_A = SGU_GROUPS * LANES
MIX_B = N_Q_HEADS * HEAD_DIM
KV_W = N_KV_HEADS * HEAD_DIM
N_VARIANTS = 4

FFN_ROWS = 1024
FFN_CHUNK_TILES = (6, 5)
MIX_IN_ROWS = 1024
MIX_OUT_ROWS = 1024
OUT_PROJ_GROUPS = (2, 2, 2, 2)


def _rms_normed(x, gain):
    w = x.shape[-1]
    inv = lax.rsqrt(jnp.sum(x * x, axis=-1, keepdims=True) + w * EPS)
    return (x * inv) * (gain * math.sqrt(w))


def _ffn_kernel(*refs, d_ff, n_casts, layer):
    x_ref, g_ref, win_hbm, wout_hbm = refs[:4]
    src_refs = refs[4:4 + n_casts]
    o_ref = refs[4 + n_casts]
    dst_refs = refs[5 + n_casts:5 + 2 * n_casts]
    win_ref, wout_ref, stage_in_ref, stage_out_ref, sem = refs[5 + 2 * n_casts:]
    assert sum(FFN_CHUNK_TILES) * MXU_DIM == d_ff
    n_tiles = d_ff // MXU_DIM
    step = pl.program_id(0)

    for src_ref, dst_ref in zip(src_refs, dst_refs):
        dst_ref[...] = src_ref[...].astype(BF16)

    def normed_input():
        return _rms_normed(x_ref[...], g_ref[layer:layer + 1, :]).astype(BF16)

    def swiglu_part(h, w_gate, w_up, w_down):
        gate = jnp.dot(h, w_gate, preferred_element_type=F32)
        up = jnp.dot(h, w_up, preferred_element_type=F32)
        a = (gate * jax.nn.sigmoid(gate) * up).astype(BF16)
        return jnp.dot(a, w_down, preferred_element_type=F32)

    def accumulate(first, part):
        if first:
            o_ref[...] = x_ref[...] + 0.5 * part
        else:
            o_ref[...] += 0.5 * part

    def tile_copies(c, slot):
        lo = c * MXU_DIM
        return [
            pltpu.make_async_copy(win_hbm.at[layer, :, pl.ds(lo, MXU_DIM)],
                                  stage_in_ref.at[slot, :, pl.ds(0, MXU_DIM)], sem.at[slot, 0]),
            pltpu.make_async_copy(win_hbm.at[layer, :, pl.ds(d_ff + lo, MXU_DIM)],
                                  stage_in_ref.at[slot, :, pl.ds(MXU_DIM, MXU_DIM)], sem.at[slot, 1]),
            pltpu.make_async_copy(wout_hbm.at[layer, pl.ds(lo, MXU_DIM), :],
                                  stage_out_ref.at[slot], sem.at[slot, 2]),
        ]

    @pl.when(step == 0)
    def _():
        for n, cp in enumerate(tile_copies(0, 0)):
            cp.start(priority=n % 2)
        for c in range(n_tiles):
            slot = c % 2
            if c + 1 < n_tiles:
                for n, cp in enumerate(tile_copies(c + 1, 1 - slot)):
                    cp.start(priority=n % 2)
            for cp in tile_copies(c, slot):
                cp.wait()
            lo = c * MXU_DIM
            w_gate = stage_in_ref[slot, :, 0:MXU_DIM].astype(BF16)
            w_up = stage_in_ref[slot, :, MXU_DIM:].astype(BF16)
            w_down = stage_out_ref[slot].astype(BF16)
            win_ref[:, lo:lo + MXU_DIM] = w_gate
            win_ref[:, d_ff + lo:d_ff + lo + MXU_DIM] = w_up
            wout_ref[lo:lo + MXU_DIM, :] = w_down

    h = normed_input()
    y = None
    lo = 0
    for tiles in FFN_CHUNK_TILES:
        tf = tiles * MXU_DIM
        part = swiglu_part(h, win_ref[:, lo:lo + tf], win_ref[:, d_ff + lo:d_ff + lo + tf],
                           wout_ref[lo:lo + tf, :])
        y = part if y is None else y + part
        lo += tf
    o_ref[...] = x_ref[...] + 0.5 * y


def _ffn(x, gain, w_in, w_out, layer, casts=()):
    t, d = x.shape
    d_ff = w_out.shape[1]
    steps = t // FFN_ROWS
    cast_in_specs, cast_out_specs, cast_out_shapes = [], [], []
    for w, l in casts:
        _, rows, cols = w.shape
        slab = rows // steps
        assert slab * steps == rows and slab % 16 == 0
        cast_in_specs.append(pl.BlockSpec((None, slab, cols), lambda i, l=l: (l, i, 0)))
        cast_out_specs.append(pl.BlockSpec((slab, cols), lambda i: (i, 0)))
        cast_out_shapes.append(jax.ShapeDtypeStruct((rows, cols), BF16))
    outs = pl.pallas_call(
        functools.partial(_ffn_kernel, d_ff=d_ff, n_casts=len(casts), layer=layer),
        grid=(steps,),
        in_specs=[
            pl.BlockSpec((FFN_ROWS, d), lambda i: (i, 0)),
            pl.BlockSpec(gain.shape, lambda i: (0, 0)),
            pl.BlockSpec(memory_space=pl.ANY),
            pl.BlockSpec(memory_space=pl.ANY),
        ] + cast_in_specs,
        out_specs=[pl.BlockSpec((FFN_ROWS, d), lambda i: (i, 0))] + cast_out_specs,
        out_shape=[jax.ShapeDtypeStruct((t, d), F32)] + cast_out_shapes,
        scratch_shapes=[
            pltpu.VMEM((d, 2 * d_ff), BF16),
            pltpu.VMEM((d_ff, d), BF16),
            pltpu.VMEM((2, d, 2 * MXU_DIM), F32),
            pltpu.VMEM((2, MXU_DIM, d), F32),
            pltpu.SemaphoreType.DMA((2, 3)),
        ],
        compiler_params=pltpu.CompilerParams(
            dimension_semantics=("arbitrary",), vmem_limit_bytes=VMEM_LIMIT_BYTES),
        name="ffn",
    )(x, gain, w_in, w_out, *[w for w, _ in casts])
    return outs[0], outs[1:]


def _gelu_tanh(x):
    k0 = -2.0 * math.sqrt(2.0 / math.pi) * math.log2(math.e)
    k1 = k0 * 0.044715
    return x / (1.0 + jnp.exp2(x * (k0 + k1 * (x * x))))


def _segment_sumsq(x):
    x2 = x * x
    low = lax.broadcasted_iota(jnp.int32, (x.shape[0], LANES), 1) < HEAD_DIM
    parts = []
    for c in range(0, x.shape[-1], LANES):
        t = x2[:, c:c + LANES]
        s_low = jnp.sum(jnp.where(low, t, 0.0), axis=-1, keepdims=True)
        s_high = jnp.sum(jnp.where(low, 0.0, t), axis=-1, keepdims=True)
        parts.append(jnp.where(low, s_low, s_high))
    return parts[0] if len(parts) == 1 else jnp.concatenate(parts, axis=-1)


def _lane_half_variants(x):
    packed = pltpu.bitcast(x, jnp.uint32)
    swapped = pltpu.bitcast(pltpu.roll(packed, HEAD_DIM, axis=1), x.dtype)
    low = lax.broadcasted_iota(jnp.int32, x.shape, 1) < HEAD_DIM
    zero = jnp.zeros_like(x)
    return [
        jnp.where(low, x, zero),
        jnp.where(low, zero, swapped),
        jnp.where(low, swapped, zero),
        jnp.where(low, zero, x),
    ]


def _mix_in_kernel(x_ref, g_ref, win_ref, sgu_g_ref, sgu_w_ref, sgu_b_ref, qg_ref, kg_ref,
                   outa_ref, q_ref, k_ref, v_ref, *, layer):
    gain_row = lambda ref: ref[layer:layer + 1, :]
    x = x_ref[...]
    rows = x.shape[0]
    h = _rms_normed(x, gain_row(g_ref)).astype(BF16)
    p = jnp.dot(h, win_ref[...], preferred_element_type=F32)

    z = _gelu_tanh(p[:, :2 * MIX_A])
    for g in range(SGU_GROUPS):
        u = z[:, g * LANES:(g + 1) * LANES]
        v = z[:, MIX_A + g * LANES:MIX_A + (g + 1) * LANES]
        vn = _rms_normed(v, sgu_g_ref[layer:layer + 1, g * LANES:(g + 1) * LANES]).astype(BF16)
        w_s = sgu_w_ref[g].astype(BF16)
        b_s = sgu_b_ref[g]
        chunks = [slice(c * CHUNK, (c + 1) * CHUNK) for c in range(rows // CHUNK)]
        s = jnp.dot(w_s, jnp.concatenate([vn[r] for r in chunks], axis=1),
                    preferred_element_type=F32)
        for c, r in enumerate(chunks):
            outa_ref[r, g * LANES:(g + 1) * LANES] = (
                u[r] * (s[:, c * LANES:(c + 1) * LANES] + b_s)).astype(BF16)

    q = p[:, 2 * MIX_A:2 * MIX_A + MIX_B]
    q_inv = lax.rsqrt(_segment_sumsq(q) + HEAD_DIM * EPS)
    q_ref[...] = ((q * q_inv) * gain_row(qg_ref)).astype(BF16)

    k = p[:, 2 * MIX_A + MIX_B:2 * MIX_A + MIX_B + KV_W]
    k_inv = lax.rsqrt(_segment_sumsq(k) + HEAD_DIM * EPS)
    k_ref[...] = ((k * k_inv) * (gain_row(kg_ref) * math.sqrt(HEAD_DIM))).astype(BF16)
    v_ref[...] = p[:, 2 * MIX_A + MIX_B + KV_W:].astype(BF16)


def _mix_in(x, gain, w_in, sgu_g, sgu_w, sgu_b, q_g, k_g, layer):
    t, d = x.shape
    in_cols = w_in.shape[-1]
    rows = MIX_IN_ROWS
    row_spec = lambda w: pl.BlockSpec((rows, w), lambda i: (i, 0))
    layer_spec = lambda *shape: pl.BlockSpec((None,) + shape, lambda i: (layer,) + (0,) * len(shape))
    whole = lambda a: pl.BlockSpec(a.shape, lambda i: (0,) * a.ndim)
    return pl.pallas_call(
        functools.partial(_mix_in_kernel, layer=layer),
        grid=(t // rows,),
        in_specs=[
            row_spec(d),
            whole(gain),
            pl.BlockSpec((d, in_cols), lambda i: (0, 0), pipeline_mode=pl.Buffered(1)),
            whole(sgu_g),
            layer_spec(SGU_GROUPS, CHUNK, CHUNK),
            layer_spec(SGU_GROUPS, CHUNK, LANES),
            whole(q_g),
            whole(k_g),
        ],
        out_specs=[row_spec(MIX_A), row_spec(MIX_B), row_spec(KV_W), row_spec(KV_W)],
        out_shape=[
            jax.ShapeDtypeStruct((t, MIX_A), BF16),
            jax.ShapeDtypeStruct((t, MIX_B), BF16),
            jax.ShapeDtypeStruct((t, KV_W), BF16),
            jax.ShapeDtypeStruct((t, KV_W), BF16),
        ],
        compiler_params=pltpu.CompilerParams(
            dimension_semantics=("parallel",), vmem_limit_bytes=VMEM_LIMIT_BYTES),
        name="mix_in",
    )(x, gain, w_in, sgu_g, sgu_w, sgu_b, q_g, k_g)


HALF_W = 3 * CHUNK
KV_COLS = 2 * LANES


def _mix_out_kernel(sink_ref, x_ref, outa_ref, q_ref, k_ref, v_ref, bias_ref, wout_ref,
                    o_ref, kpad_ref, vpad_ref, *, layer, n_blocks):
    j = pl.program_id(1)
    seq = k_ref.shape[0]

    @pl.when(j == 0)
    def _():
        low = lax.broadcasted_iota(jnp.int32, (CHUNK, LANES), 1) < HEAD_DIM
        ones_low = jnp.where(low, 1.0, 0.0).astype(BF16)
        ones_high = jnp.where(low, 0.0, 1.0).astype(BF16)
        kpad_ref[0:CHUNK, :] = jnp.zeros((CHUNK, N_VARIANTS * LANES), BF16)
        kpad_ref[CHUNK + seq:, :] = jnp.zeros((CHUNK, N_VARIANTS * LANES), BF16)
        k_variants = _lane_half_variants(k_ref[...])
        v_variants = _lane_half_variants(v_ref[...])
        for vi in range(N_VARIANTS):
            kpad_ref[CHUNK:CHUNK + seq, vi * LANES:(vi + 1) * LANES] = k_variants[vi]
            c0 = 2 * vi * LANES
            vpad_ref[0:CHUNK, c0:c0 + LANES] = jnp.zeros((CHUNK, LANES), BF16)
            vpad_ref[CHUNK:CHUNK + seq, c0:c0 + LANES] = v_variants[vi]
            vpad_ref[CHUNK + seq:, c0:c0 + LANES] = jnp.zeros((CHUNK, LANES), BF16)
            ones = ones_low if vi % 2 == 0 else ones_high
            for r0 in range(0, seq + 2 * CHUNK, CHUNK):
                vpad_ref[r0:r0 + CHUNK, c0 + LANES:c0 + 2 * LANES] = ones

    blocks_per_step = x_ref.shape[0] // CHUNK
    first_rows = lax.broadcasted_iota(jnp.int32, (2 * CHUNK, 1), 0) < CHUNK
    low_lanes = lax.broadcasted_iota(jnp.int32, (2 * CHUNK, LANES), 1) < HEAD_DIM
    sink_cols = [[jnp.where(first_rows, sink_ref[layer, 4 * kvh + half], sink_ref[layer, 4 * kvh + half + 2])
                  for half in range(2)] for kvh in range(N_KV_HEADS)]

    def scores(i, kvh):
        n = j * blocks_per_step + i
        edge = jnp.where(n == 0, 0, jnp.where(n == n_blocks - 1, 2, 1))
        win = pl.ds(pl.multiple_of(n * CHUNK, CHUNK), HALF_W)
        r = slice(i * CHUNK, (i + 1) * CHUNK)
        cg = 2 * kvh
        lhs = jnp.concatenate([q_ref[r, cg * LANES:(cg + 1) * LANES],
                               q_ref[r, (cg + 1) * LANES:(cg + 2) * LANES]], axis=0)
        keys = jnp.concatenate([kpad_ref[win, kvh * KV_COLS:kvh * KV_COLS + LANES],
                                kpad_ref[win, kvh * KV_COLS + LANES:(kvh + 1) * KV_COLS]], axis=0)
        s = lax.dot_general(lhs, keys, (((1,), (1,)), ((), ())), preferred_element_type=F32)
        s = s + bias_ref[edge, kvh]
        m = [jnp.maximum(jnp.max(s[:, h * HALF_W:(h + 1) * HALF_W], axis=-1, keepdims=True),
                         sink_cols[kvh][h]) for h in range(2)]
        return s, m, win

    def outputs(kvh, s, m, win):
        e = jnp.concatenate([jnp.exp(s[:, h * HALF_W:(h + 1) * HALF_W] - m[h]) for h in range(2)],
                            axis=-1).astype(BF16)
        vals = jnp.concatenate([vpad_ref[win, 2 * kvh * KV_COLS:(2 * kvh + 1) * KV_COLS],
                                vpad_ref[win, (2 * kvh + 1) * KV_COLS:(2 * kvh + 2) * KV_COLS]], axis=0)
        pv = jnp.dot(e, vals, preferred_element_type=F32)
        e_sink = jnp.where(low_lanes, jnp.exp(sink_cols[kvh][0] - m[0]),
                           jnp.exp(sink_cols[kvh][1] - m[1]))
        o_kv = pv[:, :LANES] * (1.0 / (pv[:, LANES:] + e_sink))
        return [o_kv[0:CHUNK].astype(BF16), o_kv[CHUNK:].astype(BF16)]

    work = [(i, kvh) for i in range(blocks_per_step) for kvh in range(N_KV_HEADS)]
    per_block = 2 * N_KV_HEADS
    assert sum(OUT_PROJ_GROUPS) == blocks_per_step
    group_ends = set(np.cumsum(OUT_PROJ_GROUPS).tolist())
    tiles = []
    proj_pieces = []

    def proj_piece(rows, mixed, c):
        cols = slice(c * MXU_DIM, (c + 1) * MXU_DIM)
        o_ref[rows, cols] = x_ref[rows, cols] + jnp.dot(mixed, wout_ref[:, cols],
                                                        preferred_element_type=F32)

    pending = scores(*work[0])
    for idx, (i, kvh) in enumerate(work):
        nxt = scores(*work[idx + 1]) if idx + 1 < len(work) else None
        tiles += outputs(kvh, *pending)
        pending = nxt
        if proj_pieces:
            proj_piece(*proj_pieces.pop(0))
        if kvh == N_KV_HEADS - 1 and (i + 1) in group_ends:
            n_group = len(tiles) // per_block
            rows = slice((i + 1 - n_group) * CHUNK, (i + 1) * CHUNK)
            attn = jnp.concatenate(
                [jnp.concatenate(tiles[b * per_block:(b + 1) * per_block], axis=-1)
                 for b in range(n_group)], axis=0)
            tiles = []
            mixed = jnp.concatenate([outa_ref[rows, :], attn], axis=-1)
            proj_pieces += [(rows, mixed, c) for c in range(o_ref.shape[1] // MXU_DIM)]
    for piece in proj_pieces:
        proj_piece(*piece)


def _mix_out(x, outa, q, k, v, sink, bias_tab, w_out, layer, batch):
    t, d = x.shape
    seq = t // batch
    rows = MIX_OUT_ROWS
    steps = seq // rows
    resident = pl.Buffered(1)
    row_spec = lambda w: pl.BlockSpec((rows, w), lambda b, j: (b * steps + j, 0))
    seq_spec = pl.BlockSpec((seq, KV_W), lambda b, j: (b, 0))
    return pl.pallas_call(
        functools.partial(_mix_out_kernel, layer=layer, n_blocks=seq // CHUNK),
        grid=(batch, steps),
        in_specs=[
            pl.BlockSpec(memory_space=pltpu.SMEM),
            row_spec(d), row_spec(MIX_A), row_spec(MIX_B), seq_spec, seq_spec,
            pl.BlockSpec(bias_tab.shape, lambda b, j: (0, 0, 0, 0), pipeline_mode=resident),
            pl.BlockSpec((MIX_A + MIX_B, d), lambda b, j: (0, 0), pipeline_mode=resident),
        ],
        out_specs=row_spec(d),
        out_shape=jax.ShapeDtypeStruct((t, d), F32),
        scratch_shapes=[
            pltpu.VMEM((seq + 2 * CHUNK, N_VARIANTS * LANES), BF16),
            pltpu.VMEM((seq + 2 * CHUNK, 2 * N_VARIANTS * LANES), BF16),
        ],
        compiler_params=pltpu.CompilerParams(
            dimension_semantics=("parallel", "arbitrary"), vmem_limit_bytes=VMEM_LIMIT_BYTES),
        name="mix_out",
    )(sink, x, outa, q, k, v, bias_tab, w_out)


def _t5_bucket(rel):
    nb = N_BUCKETS // 2
    ret = (rel > 0).astype(np.int32) * nb
    n = np.abs(rel)
    max_exact = nb // 2
    large = max_exact + (np.log(np.maximum(n, 1).astype(np.float32) / max_exact)
                         / math.log(MAX_DISTANCE / max_exact) * (nb - max_exact)).astype(np.int32)
    large = np.minimum(large, nb - 1)
    return ret + np.where(n < max_exact, n, large).astype(np.int32)


def _bias_table(rel_bias):
    rel = np.arange(3 * CHUNK)[None, :] - CHUNK - np.arange(CHUNK)[:, None]
    bucket = jnp.asarray(_t5_bucket(rel), jnp.int32)
    rb = rel_bias.astype(F32)
    bias = jnp.zeros((N_Q_HEADS,) + rel.shape, F32)
    for b in range(N_BUCKETS):
        bias = jnp.where((bucket == b)[None], rb[b][:, None, None], bias)
    band = np.abs(rel) <= CHUNK
    col = np.arange(3 * CHUNK)[None, :]
    edge_masks = [band & (col >= CHUNK), band, band & (col < 2 * CHUNK)]
    allowed = jnp.asarray(sum(m.astype(np.int32) << e for e, m in enumerate(edge_masks)), jnp.int32)
    allowed = jnp.concatenate([allowed, allowed], axis=0)
    head_rows = np.array([[4 * (vi // 2) + (vi % 2), 4 * (vi // 2) + (vi % 2) + 2]
                          for vi in range(N_VARIANTS)])
    per_variant = jnp.stack([jnp.concatenate([bias[a], bias[b]], axis=0) for a, b in head_rows])
    tabs = [jnp.where(((allowed >> e) & 1)[None] == 1, per_variant, NEG_INF)
            for e in range(len(edge_masks))]
    tab = jnp.stack(tabs, axis=0)
    return jnp.concatenate([tab[:, 0::2], tab[:, 1::2]], axis=-1)


def kernel(x, ffn1_norm, ffn1_w_in, ffn1_w_out, mix_norm, w_mix_in, sgu_norm, sgu_w, sgu_b,
           q_norm, k_norm, sink, rel_bias, w_mix_out, ffn2_norm, ffn2_w_in, ffn2_w_out):
    batch, seq, d = x.shape
    depth = ffn1_norm.shape[0]
    assert seq % MIX_OUT_ROWS == 0 and (batch * seq) % FFN_ROWS == 0

    f32 = lambda a: a.astype(F32)
    ffn1_g, mix_g, ffn2_g, sgu_g = f32(ffn1_norm), f32(mix_norm), f32(ffn2_norm), f32(sgu_norm)
    q_g = jnp.tile(f32(q_norm), (1, N_Q_HEADS))
    k_g = jnp.tile(f32(k_norm), (1, N_KV_HEADS))
    sgu_b3 = jnp.broadcast_to(sgu_b.astype(F32)[..., None], sgu_b.shape + (LANES,))
    bias_tab = _bias_table(rel_bias)
    sink = sink.astype(F32)
    xt = x.reshape(batch * seq, d)
    for l in range(depth):
        xt, (mix_wi, mix_wo) = _ffn(xt, ffn1_g, ffn1_w_in, ffn1_w_out, l,
                                    [(w_mix_in, l), (w_mix_out, l)])
        outa, q, k, v = _mix_in(xt, mix_g, mix_wi, sgu_g, sgu_w, sgu_b3, q_g, k_g, l)
        xt = _mix_out(xt, outa, q, k, v, sink, bias_tab, mix_wo, l, batch)
        xt, _ = _ffn(xt, ffn2_g, ffn2_w_in, ffn2_w_out, l)
    return xt.reshape(batch, seq, d)
```

```python
import functools
import math

import jax
import jax.numpy as jnp
import numpy as np
from jax import lax
from jax.experimental import pallas as pl
from jax.experimental.pallas import tpu as pltpu

F32 = jnp.float32
BF16 = jnp.bfloat16

EPS = 1e-6
NEG_INF = -1e30

LANES = 128
MXU_DIM = 256
VMEM_LIMIT_BYTES = 56 * 1024 * 1024

SGU_GROUPS = 4
CHUNK = 128
N_Q_HEADS = 8
N_KV_HEADS = 2
HEAD_DIM = 64
N_BUCKETS = 32
MAX_DISTANCE = 128
MIX_A = SGU_GROUPS * LANES
MIX_B = N_Q_HEADS * HEAD_DIM
KV_W = N_KV_HEADS * HEAD_DIM
N_VARIANTS = 4

FFN_ROWS = 1024
FFN_CHUNK_TILES = (6, 5)
MIX_IN_ROWS = 1024
MIX_OUT_ROWS = 1024
OUT_PROJ_GROUPS = (2, 2, 2, 2)


def _rms_normed(x, gain):
    w = x.shape[-1]
    inv = lax.rsqrt(jnp.sum(x * x, axis=-1, keepdims=True) + w * EPS)
    return (x * inv) * (gain * math.sqrt(w))


def _ffn_kernel(*refs, d_ff, n_casts, layer):
    x_ref, g_ref, win_ref, wout_ref = refs[:4]
    src_refs = refs[4:4 + n_casts]
    o_ref = refs[4 + n_casts]
    dst_refs = refs[5 + n_casts:]
    for src_ref, dst_ref in zip(src_refs, dst_refs):
        dst_ref[...] = src_ref[...].astype(BF16)
    x = x_ref[...]
    h = _rms_normed(x, g_ref[layer:layer + 1, :]).astype(BF16)
    assert sum(FFN_CHUNK_TILES) * MXU_DIM == d_ff
    y = None
    lo = 0
    for tiles in FFN_CHUNK_TILES:
        tf = tiles * MXU_DIM
        gate = jnp.dot(h, win_ref[:, lo:lo + tf], preferred_element_type=F32)
        up = jnp.dot(h, win_ref[:, d_ff + lo:d_ff + lo + tf], preferred_element_type=F32)
        a = (gate * jax.nn.sigmoid(gate) * up).astype(BF16)
        part = jnp.dot(a, wout_ref[lo:lo + tf, :], preferred_element_type=F32)
        y = part if y is None else y + part
        lo += tf
    o_ref[...] = x + 0.5 * y


def _ffn(x, gain, w_in, w_out, layer, casts=()):
    t, d = x.shape
    d_ff = w_out.shape[0]
    steps = t // FFN_ROWS
    resident = pl.Buffered(1)
    cast_in_specs, cast_out_specs, cast_out_shapes = [], [], []
    for w, l in casts:
        _, rows, cols = w.shape
        slab = rows // steps
        assert slab * steps == rows and slab % 16 == 0
        cast_in_specs.append(pl.BlockSpec((None, slab, cols), lambda i, l=l: (l, i, 0)))
        cast_out_specs.append(pl.BlockSpec((slab, cols), lambda i: (i, 0)))
        cast_out_shapes.append(jax.ShapeDtypeStruct((rows, cols), BF16))
    outs = pl.pallas_call(
        functools.partial(_ffn_kernel, d_ff=d_ff, n_casts=len(casts), layer=layer),
        grid=(steps,),
        in_specs=[
            pl.BlockSpec((FFN_ROWS, d), lambda i: (i, 0)),
            pl.BlockSpec(gain.shape, lambda i: (0, 0)),
            pl.BlockSpec((d, 2 * d_ff), lambda i: (0, 0), pipeline_mode=resident),
            pl.BlockSpec((d_ff, d), lambda i: (0, 0), pipeline_mode=resident),
        ] + cast_in_specs,
        out_specs=[pl.BlockSpec((FFN_ROWS, d), lambda i: (i, 0))] + cast_out_specs,
        out_shape=[jax.ShapeDtypeStruct((t, d), F32)] + cast_out_shapes,
        compiler_params=pltpu.CompilerParams(
            dimension_semantics=("parallel",), vmem_limit_bytes=VMEM_LIMIT_BYTES),
        name="ffn",
    )(x, gain, w_in, w_out, *[w for w, _ in casts])
    return outs[0], outs[1:]


def _gelu_tanh(x):
    k0 = -2.0 * math.sqrt(2.0 / math.pi) * math.log2(math.e)
    k1 = k0 * 0.044715
    return x / (1.0 + jnp.exp2(x * (k0 + k1 * (x * x))))


def _segment_sumsq(x):
    x2 = x * x
    low = lax.broadcasted_iota(jnp.int32, (x.shape[0], LANES), 1) < HEAD_DIM
    parts = []
    for c in range(0, x.shape[-1], LANES):
        t = x2[:, c:c + LANES]
        s_low = jnp.sum(jnp.where(low, t, 0.0), axis=-1, keepdims=True)
        s_high = jnp.sum(jnp.where(low, 0.0, t), axis=-1, keepdims=True)
        parts.append(jnp.where(low, s_low, s_high))
    return parts[0] if len(parts) == 1 else jnp.concatenate(parts, axis=-1)


def _lane_half_variants(x):
    packed = pltpu.bitcast(x, jnp.uint32)
    swapped = pltpu.bitcast(pltpu.roll(packed, HEAD_DIM, axis=1), x.dtype)
    low = lax.broadcasted_iota(jnp.int32, x.shape, 1) < HEAD_DIM
    zero = jnp.zeros_like(x)
    return [
        jnp.where(low, x, zero),
        jnp.where(low, zero, swapped),
        jnp.where(low, swapped, zero),
        jnp.where(low, zero, x),
    ]


def _mix_in_kernel(x_ref, g_ref, win_ref, sgu_g_ref, sgu_w_ref, sgu_b_ref, qg_ref, kg_ref,
                   outa_ref, q_ref, k_ref, v_ref, *, layer):
    gain_row = lambda ref: ref[layer:layer + 1, :]
    x = x_ref[...]
    rows = x.shape[0]
    h = _rms_normed(x, gain_row(g_ref)).astype(BF16)
    project = lambda lo, hi: jnp.dot(h, win_ref[:, lo:hi], preferred_element_type=F32)
    p_v = project(MIX_A, 2 * MIX_A)
    p_att = project(2 * MIX_A, win_ref.shape[1])
    p_u = project(0, MIX_A)

    zv = _gelu_tanh(p_v)
    mixed = []
    for g in range(SGU_GROUPS):
        v = zv[:, g * LANES:(g + 1) * LANES]
        vn = _rms_normed(v, sgu_g_ref[layer:layer + 1, g * LANES:(g + 1) * LANES]).astype(BF16)
        w_s = sgu_w_ref[g].astype(BF16)
        chunks = [slice(c * CHUNK, (c + 1) * CHUNK) for c in range(rows // CHUNK)]
        mixed.append(jnp.dot(w_s, jnp.concatenate([vn[r] for r in chunks], axis=1),
                             preferred_element_type=F32))

    q = p_att[:, :MIX_B]
    q_inv = lax.rsqrt(_segment_sumsq(q) + HEAD_DIM * EPS)
    q_ref[...] = ((q * q_inv) * gain_row(qg_ref)).astype(BF16)

    k = p_att[:, MIX_B:MIX_B + KV_W]
    k_inv = lax.rsqrt(_segment_sumsq(k) + HEAD_DIM * EPS)
    k_ref[...] = ((k * k_inv) * (gain_row(kg_ref) * math.sqrt(HEAD_DIM))).astype(BF16)
    v_ref[...] = p_att[:, MIX_B + KV_W:].astype(BF16)

    zu = _gelu_tanh(p_u)
    for g in range(SGU_GROUPS):
        b_s = sgu_b_ref[g]
        for c, r in enumerate(chunks):
            outa_ref[r, g * LANES:(g + 1) * LANES] = (
                zu[r, g * LANES:(g + 1) * LANES]
                * (mixed[g][:, c * LANES:(c + 1) * LANES] + b_s)).astype(BF16)


def _mix_in(x, gain, w_in, sgu_g, sgu_w, sgu_b, q_g, k_g, layer):
    t, d = x.shape
    in_cols = w_in.shape[-1]
    rows = MIX_IN_ROWS
    row_spec = lambda w: pl.BlockSpec((rows, w), lambda i: (i, 0))
    layer_spec = lambda *shape: pl.BlockSpec((None,) + shape, lambda i: (layer,) + (0,) * len(shape))
    whole = lambda a: pl.BlockSpec(a.shape, lambda i: (0,) * a.ndim)
    return pl.pallas_call(
        functools.partial(_mix_in_kernel, layer=layer),
        grid=(t // rows,),
        in_specs=[
            row_spec(d),
            whole(gain),
            pl.BlockSpec((d, in_cols), lambda i: (0, 0), pipeline_mode=pl.Buffered(1)),
            whole(sgu_g),
            layer_spec(SGU_GROUPS, CHUNK, CHUNK),
            layer_spec(SGU_GROUPS, CHUNK, LANES),
            whole(q_g),
            whole(k_g),
        ],
        out_specs=[row_spec(MIX_A), row_spec(MIX_B), row_spec(KV_W), row_spec(KV_W)],
        out_shape=[
            jax.ShapeDtypeStruct((t, MIX_A), BF16),
            jax.ShapeDtypeStruct((t, MIX_B), BF16),
            jax.ShapeDtypeStruct((t, KV_W), BF16),
            jax.ShapeDtypeStruct((t, KV_W), BF16),
        ],
        compiler_params=pltpu.CompilerParams(
            dimension_semantics=("parallel",), vmem_limit_bytes=VMEM_LIMIT_BYTES),
        name="mix_in",
    )(x, gain, w_in, sgu_g, sgu_w, sgu_b, q_g, k_g)


HALF_W = 3 * CHUNK
KV_COLS = 2 * LANES


def _mix_out_kernel(sink_ref, x_ref, outa_ref, q_ref, k_ref, v_ref, bias_ref, wout_ref,
                    o_ref, kpad_ref, vpad_ref, *, layer, n_blocks):
    j = pl.program_id(1)
    seq = k_ref.shape[0]

    @pl.when(j == 0)
    def _():
        low = lax.broadcasted_iota(jnp.int32, (CHUNK, LANES), 1) < HEAD_DIM
        ones_low = jnp.where(low, 1.0, 0.0).astype(BF16)
        ones_high = jnp.where(low, 0.0, 1.0).astype(BF16)
        kpad_ref[0:CHUNK, :] = jnp.zeros((CHUNK, N_VARIANTS * LANES), BF16)
        kpad_ref[CHUNK + seq:, :] = jnp.zeros((CHUNK, N_VARIANTS * LANES), BF16)
        k_variants = _lane_half_variants(k_ref[...])
        v_variants = _lane_half_variants(v_ref[...])
        for vi in range(N_VARIANTS):
            kpad_ref[CHUNK:CHUNK + seq, vi * LANES:(vi + 1) * LANES] = k_variants[vi]
            c0 = 2 * vi * LANES
            vpad_ref[0:CHUNK, c0:c0 + LANES] = jnp.zeros((CHUNK, LANES), BF16)
            vpad_ref[CHUNK:CHUNK + seq, c0:c0 + LANES] = v_variants[vi]
            vpad_ref[CHUNK + seq:, c0:c0 + LANES] = jnp.zeros((CHUNK, LANES), BF16)
            ones = ones_low if vi % 2 == 0 else ones_high
            for r0 in range(0, seq + 2 * CHUNK, CHUNK):
                vpad_ref[r0:r0 + CHUNK, c0 + LANES:c0 + 2 * LANES] = ones

    blocks_per_step = x_ref.shape[0] // CHUNK
    first_rows = lax.broadcasted_iota(jnp.int32, (2 * CHUNK, 1), 0) < CHUNK
    low_lanes = lax.broadcasted_iota(jnp.int32, (2 * CHUNK, LANES), 1) < HEAD_DIM
    sink_cols = [[jnp.where(first_rows, sink_ref[layer, 4 * kvh + half], sink_ref[layer, 4 * kvh + half + 2])
                  for half in range(2)] for kvh in range(N_KV_HEADS)]

    def scores(i, kvh):
        n = j * blocks_per_step + i
        edge = jnp.where(n == 0, 0, jnp.where(n == n_blocks - 1, 2, 1))
        win = pl.ds(pl.multiple_of(n * CHUNK, CHUNK), HALF_W)
        r = slice(i * CHUNK, (i + 1) * CHUNK)
        cg = 2 * kvh
        lhs = jnp.concatenate([q_ref[r, cg * LANES:(cg + 1) * LANES],
                               q_ref[r, (cg + 1) * LANES:(cg + 2) * LANES]], axis=0)
        keys = jnp.concatenate([kpad_ref[win, kvh * KV_COLS:kvh * KV_COLS + LANES],
                                kpad_ref[win, kvh * KV_COLS + LANES:(kvh + 1) * KV_COLS]], axis=0)
        s = lax.dot_general(lhs, keys, (((1,), (1,)), ((), ())), preferred_element_type=F32)
        s = s + bias_ref[edge, kvh]
        m = [jnp.maximum(jnp.max(s[:, h * HALF_W:(h + 1) * HALF_W], axis=-1, keepdims=True),
                         sink_cols[kvh][h]) for h in range(2)]
        return s, m, win

    def outputs(kvh, s, m, win):
        e = jnp.concatenate([jnp.exp(s[:, h * HALF_W:(h + 1) * HALF_W] - m[h]) for h in range(2)],
                            axis=-1).astype(BF16)
        vals = jnp.concatenate([vpad_ref[win, 2 * kvh * KV_COLS:(2 * kvh + 1) * KV_COLS],
                                vpad_ref[win, (2 * kvh + 1) * KV_COLS:(2 * kvh + 2) * KV_COLS]], axis=0)
        pv = jnp.dot(e, vals, preferred_element_type=F32)
        e_sink = jnp.where(low_lanes, jnp.exp(sink_cols[kvh][0] - m[0]),
                           jnp.exp(sink_cols[kvh][1] - m[1]))
        o_kv = pv[:, :LANES] * (1.0 / (pv[:, LANES:] + e_sink))
        return [o_kv[0:CHUNK].astype(BF16), o_kv[CHUNK:].astype(BF16)]

    work = [(i, kvh) for i in range(blocks_per_step) for kvh in range(N_KV_HEADS)]
    per_block = 2 * N_KV_HEADS
    assert sum(OUT_PROJ_GROUPS) == blocks_per_step
    group_ends = set(np.cumsum(OUT_PROJ_GROUPS).tolist())
    tiles = []
    proj_pieces = []

    def proj_piece(rows, mixed, c):
        cols = slice(c * MXU_DIM, (c + 1) * MXU_DIM)
        o_ref[rows, cols] = x_ref[rows, cols] + jnp.dot(mixed, wout_ref[:, cols],
                                                        preferred_element_type=F32)

    pending = scores(*work[0])
    for idx, (i, kvh) in enumerate(work):
        nxt = scores(*work[idx + 1]) if idx + 1 < len(work) else None
        tiles += outputs(kvh, *pending)
        pending = nxt
        if proj_pieces:
            proj_piece(*proj_pieces.pop(0))
        if kvh == N_KV_HEADS - 1 and (i + 1) in group_ends:
            n_group = len(tiles) // per_block
            rows = slice((i + 1 - n_group) * CHUNK, (i + 1) * CHUNK)
            attn = jnp.concatenate(
                [jnp.concatenate(tiles[b * per_block:(b + 1) * per_block], axis=-1)
                 for b in range(n_group)], axis=0)
            tiles = []
            mixed = jnp.concatenate([outa_ref[rows, :], attn], axis=-1)
            proj_pieces += [(rows, mixed, c) for c in range(o_ref.shape[1] // MXU_DIM)]
    for piece in proj_pieces:
        proj_piece(*piece)


def _mix_out(x, outa, q, k, v, sink, bias_tab, w_out, layer, batch):
    t, d = x.shape
    seq = t // batch
    rows = MIX_OUT_ROWS
    steps = seq // rows
    resident = pl.Buffered(1)
    row_spec = lambda w: pl.BlockSpec((rows, w), lambda b, j: (b * steps + j, 0))
    seq_spec = pl.BlockSpec((seq, KV_W), lambda b, j: (b, 0))
    return pl.pallas_call(
        functools.partial(_mix_out_kernel, layer=layer, n_blocks=seq // CHUNK),
        grid=(batch, steps),
        in_specs=[
            pl.BlockSpec(memory_space=pltpu.SMEM),
            row_spec(d), row_spec(MIX_A), row_spec(MIX_B), seq_spec, seq_spec,
            pl.BlockSpec(bias_tab.shape, lambda b, j: (0, 0, 0, 0), pipeline_mode=resident),
            pl.BlockSpec((MIX_A + MIX_B, d), lambda b, j: (0, 0), pipeline_mode=resident),
        ],
        out_specs=row_spec(d),
        out_shape=jax.ShapeDtypeStruct((t, d), F32),
        scratch_shapes=[
            pltpu.VMEM((seq + 2 * CHUNK, N_VARIANTS * LANES), BF16),
            pltpu.VMEM((seq + 2 * CHUNK, 2 * N_VARIANTS * LANES), BF16),
        ],
        compiler_params=pltpu.CompilerParams(
            dimension_semantics=("parallel", "arbitrary"), vmem_limit_bytes=VMEM_LIMIT_BYTES),
        name="mix_out",
    )(sink, x, outa, q, k, v, bias_tab, w_out)


def _t5_bucket(rel):
    nb = N_BUCKETS // 2
    ret = (rel > 0).astype(np.int32) * nb
    n = np.abs(rel)
    max_exact = nb // 2
    large = max_exact + (np.log(np.maximum(n, 1).astype(np.float32) / max_exact)
                         / math.log(MAX_DISTANCE / max_exact) * (nb - max_exact)).astype(np.int32)
    large = np.minimum(large, nb - 1)
    return ret + np.where(n < max_exact, n, large).astype(np.int32)


def _bias_table(rel_bias):
    rel = np.arange(3 * CHUNK)[None, :] - CHUNK - np.arange(CHUNK)[:, None]
    bucket = jnp.asarray(_t5_bucket(rel), jnp.int32)
    rb = rel_bias.astype(F32)
    bias = jnp.zeros((N_Q_HEADS,) + rel.shape, F32)
    for b in range(N_BUCKETS):
        bias = jnp.where((bucket == b)[None], rb[b][:, None, None], bias)
    band = np.abs(rel) <= CHUNK
    col = np.arange(3 * CHUNK)[None, :]
    edge_masks = [band & (col >= CHUNK), band, band & (col < 2 * CHUNK)]
    allowed = jnp.asarray(sum(m.astype(np.int32) << e for e, m in enumerate(edge_masks)), jnp.int32)
    allowed = jnp.concatenate([allowed, allowed], axis=0)
    head_rows = np.array([[4 * (vi // 2) + (vi % 2), 4 * (vi // 2) + (vi % 2) + 2]
                          for vi in range(N_VARIANTS)])
    per_variant = jnp.stack([jnp.concatenate([bias[a], bias[b]], axis=0) for a, b in head_rows])
    tabs = [jnp.where(((allowed >> e) & 1)[None] == 1, per_variant, NEG_INF)
            for e in range(len(edge_masks))]
    tab = jnp.stack(tabs, axis=0)
    return jnp.concatenate([tab[:, 0::2], tab[:, 1::2]], axis=-1)


def kernel(x, ffn1_norm, ffn1_w_in, ffn1_w_out, mix_norm, w_mix_in, sgu_norm, sgu_w, sgu_b,
           q_norm, k_norm, sink, rel_bias, w_mix_out, ffn2_norm, ffn2_w_in, ffn2_w_out):
    batch, seq, d = x.shape
    depth = ffn1_norm.shape[0]
    assert seq % MIX_OUT_ROWS == 0 and (batch * seq) % FFN_ROWS == 0

    f32 = lambda a: a.astype(F32)
    ffn1_g, mix_g, ffn2_g, sgu_g = f32(ffn1_norm), f32(mix_norm), f32(ffn2_norm), f32(sgu_norm)
    q_g = jnp.tile(f32(q_norm), (1, N_Q_HEADS))
    k_g = jnp.tile(f32(k_norm), (1, N_KV_HEADS))
    sgu_b3 = jnp.broadcast_to(sgu_b.astype(F32)[..., None], sgu_b.shape + (LANES,))
    bias_tab = _bias_table(rel_bias)
    sink = sink.astype(F32)
    wi, wo = ffn1_w_in[0].astype(BF16), ffn1_w_out[0].astype(BF16)

    xt = x.reshape(batch * seq, d)
    for l in range(depth):
        xt, (mix_wi, mix_wo, wi, wo) = _ffn(
            xt, ffn1_g, wi, wo, l, [(w_mix_in, l), (w_mix_out, l), (ffn2_w_in, l), (ffn2_w_out, l)])
        outa, q, k, v = _mix_in(xt, mix_g, mix_wi, sgu_g, sgu_w, sgu_b3, q_g, k_g, l)
        xt = _mix_out(xt, outa, q, k, v, sink, bias_tab, mix_wo, l, batch)
        nxt = [(ffn1_w_in, l + 1), (ffn1_w_out, l + 1)] if l + 1 < depth else []
        xt, nxt_w = _ffn(xt, ffn2_g, wi, wo, l, nxt)
        if nxt:
            wi, wo = nxt_w
    return xt.reshape(batch, seq, d)
```

```python
import functools
import math

import jax
import jax.numpy as jnp
import numpy as np
from jax import lax
from jax.experimental import pallas as pl
from jax.experimental.pallas import tpu as pltpu

F32 = jnp.float32
BF16 = jnp.bfloat16

EPS = 1e-6
NEG_INF = -1e30

LANES = 128
BF16_SUBLANES = 16
MXU_DIM = 256
VMEM_LIMIT_BYTES = 56 * 1024 * 1024

SGU_GROUPS = 4
CHUNK = 128
N_Q_HEADS = 8
N_KV_HEADS = 2
HEAD_DIM = 64
N_BUCKETS = 32
MAX_DISTANCE = 128
MIX_A = SGU_GROUPS * LANES
MIX_B = N_Q_HEADS * HEAD_DIM
KV_W = N_KV_HEADS * HEAD_DIM
N_VARIANTS = 4

FFN_ROWS = 1024
FFN_CHUNK_TILES = (6, 5)
MIX_IN_ROWS = 1024
MIX_OUT_ROWS = 1024
OUT_PROJ_GROUPS = (2, 2, 2, 2)
FIRST_DOT_ROW_PARTS = 4


def _rms_normed(x, gain):
    w = x.shape[-1]
    inv = lax.rsqrt(jnp.sum(x * x, axis=-1, keepdims=True) + w * EPS)
    return (x * inv) * (gain * math.sqrt(w))


def _row_parts_dot(h, w):
    part = h.shape[0] // FIRST_DOT_ROW_PARTS
    return jnp.concatenate([jnp.dot(h[r0:r0 + part], w, preferred_element_type=F32)
                            for r0 in range(0, h.shape[0], part)], axis=0)


def _ffn_kernel(*refs, d_ff, n_casts, layer):
    x_ref, g_ref, win_ref, wout_ref = refs[:4]
    src_refs = refs[4:4 + n_casts]
    o_ref = refs[4 + n_casts]
    dst_refs = refs[5 + n_casts:]
    for src_ref, dst_ref in zip(src_refs, dst_refs):
        dst_ref[...] = src_ref[...].astype(BF16)
    x = x_ref[...]
    h = _rms_normed(x, g_ref[layer:layer + 1, :]).astype(BF16)
    assert sum(FFN_CHUNK_TILES) * MXU_DIM == d_ff
    y = None
    lo = 0
    for tiles in FFN_CHUNK_TILES:
        tf = tiles * MXU_DIM
        if lo == 0:
            gate = _row_parts_dot(h, win_ref[:, lo:lo + tf])
        else:
            gate = jnp.dot(h, win_ref[:, lo:lo + tf], preferred_element_type=F32)
        up = jnp.dot(h, win_ref[:, d_ff + lo:d_ff + lo + tf], preferred_element_type=F32)
        a = (gate * jax.nn.sigmoid(gate) * up).astype(BF16)
        part = jnp.dot(a, wout_ref[lo:lo + tf, :], preferred_element_type=F32)
        y = part if y is None else y + part
        lo += tf
    o_ref[...] = x + 0.5 * y


def _ffn(x, gain, w_in, w_out, layer, casts=()):
    t, d = x.shape
    d_ff = w_out.shape[0]
    steps = t // FFN_ROWS
    resident = pl.Buffered(1)
    cast_in_specs, cast_out_specs, cast_out_shapes = [], [], []
    for w, l in casts:
        _, rows, cols = w.shape
        slab = rows // steps
        assert slab * steps == rows and slab % BF16_SUBLANES == 0
        cast_in_specs.append(pl.BlockSpec((None, slab, cols), lambda i, l=l: (l, i, 0)))
        cast_out_specs.append(pl.BlockSpec((slab, cols), lambda i: (i, 0)))
        cast_out_shapes.append(jax.ShapeDtypeStruct((rows, cols), BF16))
    outs = pl.pallas_call(
        functools.partial(_ffn_kernel, d_ff=d_ff, n_casts=len(casts), layer=layer),
        grid=(steps,),
        in_specs=[
            pl.BlockSpec((FFN_ROWS, d), lambda i: (i, 0)),
            pl.BlockSpec(gain.shape, lambda i: (0, 0)),
            pl.BlockSpec((d, 2 * d_ff), lambda i: (0, 0), pipeline_mode=resident),
            pl.BlockSpec((d_ff, d), lambda i: (0, 0), pipeline_mode=resident),
        ] + cast_in_specs,
        out_specs=[pl.BlockSpec((FFN_ROWS, d), lambda i: (i, 0))] + cast_out_specs,
        out_shape=[jax.ShapeDtypeStruct((t, d), F32)] + cast_out_shapes,
        compiler_params=pltpu.CompilerParams(
            dimension_semantics=("parallel",), vmem_limit_bytes=VMEM_LIMIT_BYTES),
        name="ffn",
    )(x, gain, w_in, w_out, *[w for w, _ in casts])
    return outs[0], outs[1:]


def _gelu_tanh(x):
    k0 = -2.0 * math.sqrt(2.0 / math.pi) * math.log2(math.e)
    k1 = k0 * 0.044715
    return x / (1.0 + jnp.exp2(x * (k0 + k1 * (x * x))))


def _segment_sumsq(x):
    x2 = x * x
    low = lax.broadcasted_iota(jnp.int32, (x.shape[0], LANES), 1) < HEAD_DIM
    parts = []
    for c in range(0, x.shape[-1], LANES):
        t = x2[:, c:c + LANES]
        s_low = jnp.sum(jnp.where(low, t, 0.0), axis=-1, keepdims=True)
        s_high = jnp.sum(jnp.where(low, 0.0, t), axis=-1, keepdims=True)
        parts.append(jnp.where(low, s_low, s_high))
    return parts[0] if len(parts) == 1 else jnp.concatenate(parts, axis=-1)


def _lane_half_variants(x):
    packed = pltpu.bitcast(x, jnp.uint32)
    swapped = pltpu.bitcast(pltpu.roll(packed, HEAD_DIM, axis=1), x.dtype)
    low = lax.broadcasted_iota(jnp.int32, x.shape, 1) < HEAD_DIM
    zero = jnp.zeros_like(x)
    return [
        jnp.where(low, x, zero),
        jnp.where(low, zero, swapped),
        jnp.where(low, swapped, zero),
        jnp.where(low, zero, x),
    ]


def _mix_in_kernel(x_ref, g_ref, win_ref, sgu_g_ref, sgu_w_ref, sgu_b_ref, qg_ref, kg_ref,
                   outa_ref, q_ref, k_ref, v_ref, *, layer):
    gain_row = lambda ref: ref[layer:layer + 1, :]
    x = x_ref[...]
    rows = x.shape[0]
    h = _rms_normed(x, gain_row(g_ref)).astype(BF16)
    project = lambda lo, hi: jnp.dot(h, win_ref[:, lo:hi], preferred_element_type=F32)
    p_v = project(MIX_A, 2 * MIX_A)
    p_att = project(2 * MIX_A, win_ref.shape[1])
    p_u = project(0, MIX_A)

    zv = _gelu_tanh(p_v)
    mixed = []
    for g in range(SGU_GROUPS):
        v = zv[:, g * LANES:(g + 1) * LANES]
        vn = _rms_normed(v, sgu_g_ref[layer:layer + 1, g * LANES:(g + 1) * LANES]).astype(BF16)
        w_s = sgu_w_ref[g].astype(BF16)
        chunks = [slice(c * CHUNK, (c + 1) * CHUNK) for c in range(rows // CHUNK)]
        mixed.append(jnp.dot(w_s, jnp.concatenate([vn[r] for r in chunks], axis=1),
                             preferred_element_type=F32))

    q = p_att[:, :MIX_B]
    q_inv = lax.rsqrt(_segment_sumsq(q) + HEAD_DIM * EPS)
    q_ref[...] = ((q * q_inv) * gain_row(qg_ref)).astype(BF16)

    k = p_att[:, MIX_B:MIX_B + KV_W]
    k_inv = lax.rsqrt(_segment_sumsq(k) + HEAD_DIM * EPS)
    k_ref[...] = ((k * k_inv) * (gain_row(kg_ref) * math.sqrt(HEAD_DIM))).astype(BF16)
    v_ref[...] = p_att[:, MIX_B + KV_W:].astype(BF16)

    zu = _gelu_tanh(p_u)
    for g in range(SGU_GROUPS):
        b_s = sgu_b_ref[g]
        for c, r in enumerate(chunks):
            outa_ref[r, g * LANES:(g + 1) * LANES] = (
                zu[r, g * LANES:(g + 1) * LANES]
                * (mixed[g][:, c * LANES:(c + 1) * LANES] + b_s)).astype(BF16)


def _mix_in(x, gain, w_in, sgu_g, sgu_w, sgu_b, q_g, k_g, layer):
    t, d = x.shape
    in_cols = w_in.shape[-1]
    rows = MIX_IN_ROWS
    row_spec = lambda w: pl.BlockSpec((rows, w), lambda i: (i, 0))
    layer_spec = lambda *shape: pl.BlockSpec((None,) + shape, lambda i: (layer,) + (0,) * len(shape))
    whole = lambda a: pl.BlockSpec(a.shape, lambda i: (0,) * a.ndim)
    return pl.pallas_call(
        functools.partial(_mix_in_kernel, layer=layer),
        grid=(t // rows,),
        in_specs=[
            row_spec(d),
            whole(gain),
            pl.BlockSpec((d, in_cols), lambda i: (0, 0), pipeline_mode=pl.Buffered(1)),
            whole(sgu_g),
            layer_spec(SGU_GROUPS, CHUNK, CHUNK),
            layer_spec(SGU_GROUPS, CHUNK, LANES),
            whole(q_g),
            whole(k_g),
        ],
        out_specs=[row_spec(MIX_A), row_spec(MIX_B), row_spec(KV_W), row_spec(KV_W)],
        out_shape=[
            jax.ShapeDtypeStruct((t, MIX_A), BF16),
            jax.ShapeDtypeStruct((t, MIX_B), BF16),
            jax.ShapeDtypeStruct((t, KV_W), BF16),
            jax.ShapeDtypeStruct((t, KV_W), BF16),
        ],
        compiler_params=pltpu.CompilerParams(
            dimension_semantics=("parallel",), vmem_limit_bytes=VMEM_LIMIT_BYTES),
        name="mix_in",
    )(x, gain, w_in, sgu_g, sgu_w, sgu_b, q_g, k_g)


HALF_W = 3 * CHUNK
KV_COLS = 2 * LANES


def _mix_out_kernel(sink_ref, x_ref, outa_ref, q_ref, k_ref, v_ref, bias_ref, wout_ref,
                    o_ref, kpad_ref, vpad_ref, *, layer, n_blocks):
    j = pl.program_id(1)
    seq = k_ref.shape[0]

    @pl.when(j == 0)
    def _():
        low = lax.broadcasted_iota(jnp.int32, (CHUNK, LANES), 1) < HEAD_DIM
        ones_low = jnp.where(low, 1.0, 0.0).astype(BF16)
        ones_high = jnp.where(low, 0.0, 1.0).astype(BF16)
        kpad_ref[0:CHUNK, :] = jnp.zeros((CHUNK, N_VARIANTS * LANES), BF16)
        kpad_ref[CHUNK + seq:, :] = jnp.zeros((CHUNK, N_VARIANTS * LANES), BF16)
        k_variants = _lane_half_variants(k_ref[...])
        v_variants = _lane_half_variants(v_ref[...])
        for vi in range(N_VARIANTS):
            kpad_ref[CHUNK:CHUNK + seq, vi * LANES:(vi + 1) * LANES] = k_variants[vi]
            c0 = 2 * vi * LANES
            vpad_ref[0:CHUNK, c0:c0 + LANES] = jnp.zeros((CHUNK, LANES), BF16)
            vpad_ref[CHUNK:CHUNK + seq, c0:c0 + LANES] = v_variants[vi]
            vpad_ref[CHUNK + seq:, c0:c0 + LANES] = jnp.zeros((CHUNK, LANES), BF16)
            ones = ones_low if vi % 2 == 0 else ones_high
            for r0 in range(0, seq + 2 * CHUNK, CHUNK):
                vpad_ref[r0:r0 + CHUNK, c0 + LANES:c0 + 2 * LANES] = ones

    blocks_per_step = x_ref.shape[0] // CHUNK
    first_rows = lax.broadcasted_iota(jnp.int32, (2 * CHUNK, 1), 0) < CHUNK
    low_lanes = lax.broadcasted_iota(jnp.int32, (2 * CHUNK, LANES), 1) < HEAD_DIM
    sink_cols = [[jnp.where(first_rows, sink_ref[layer, 4 * kvh + half], sink_ref[layer, 4 * kvh + half + 2])
                  for half in range(2)] for kvh in range(N_KV_HEADS)]

    def scores(i, kvh):
        n = j * blocks_per_step + i
        edge = jnp.where(n == 0, 0, jnp.where(n == n_blocks - 1, 2, 1))
        win = pl.ds(pl.multiple_of(n * CHUNK, CHUNK), HALF_W)
        r = slice(i * CHUNK, (i + 1) * CHUNK)
        cg = 2 * kvh
        lhs = jnp.concatenate([q_ref[r, cg * LANES:(cg + 1) * LANES],
                               q_ref[r, (cg + 1) * LANES:(cg + 2) * LANES]], axis=0)
        keys = jnp.concatenate([kpad_ref[win, kvh * KV_COLS:kvh * KV_COLS + LANES],
                                kpad_ref[win, kvh * KV_COLS + LANES:(kvh + 1) * KV_COLS]], axis=0)
        s = lax.dot_general(lhs, keys, (((1,), (1,)), ((), ())), preferred_element_type=F32)
        s = s + bias_ref[edge, kvh]
        m = [jnp.maximum(jnp.max(s[:, h * HALF_W:(h + 1) * HALF_W], axis=-1, keepdims=True),
                         sink_cols[kvh][h]) for h in range(2)]
        return s, m, win

    def outputs(kvh, s, m, win):
        e = jnp.concatenate([jnp.exp(s[:, h * HALF_W:(h + 1) * HALF_W] - m[h]) for h in range(2)],
                            axis=-1).astype(BF16)
        vals = jnp.concatenate([vpad_ref[win, 2 * kvh * KV_COLS:(2 * kvh + 1) * KV_COLS],
                                vpad_ref[win, (2 * kvh + 1) * KV_COLS:(2 * kvh + 2) * KV_COLS]], axis=0)
        pv = jnp.dot(e, vals, preferred_element_type=F32)
        e_sink = jnp.where(low_lanes, jnp.exp(sink_cols[kvh][0] - m[0]),
                           jnp.exp(sink_cols[kvh][1] - m[1]))
        o_kv = pv[:, :LANES] * (1.0 / (pv[:, LANES:] + e_sink))
        return [o_kv[0:CHUNK].astype(BF16), o_kv[CHUNK:].astype(BF16)]

    work = [(i, kvh) for i in range(blocks_per_step) for kvh in range(N_KV_HEADS)]
    per_block = 2 * N_KV_HEADS
    assert sum(OUT_PROJ_GROUPS) == blocks_per_step
    group_ends = set(np.cumsum(OUT_PROJ_GROUPS).tolist())
    tiles = []
    proj_pieces = []

    def proj_piece(rows, mixed, c):
        cols = slice(c * MXU_DIM, (c + 1) * MXU_DIM)
        o_ref[rows, cols] = x_ref[rows, cols] + jnp.dot(mixed, wout_ref[:, cols],
                                                        preferred_element_type=F32)

    pending = scores(*work[0])
    for idx, (i, kvh) in enumerate(work):
        nxt = scores(*work[idx + 1]) if idx + 1 < len(work) else None
        tiles += outputs(kvh, *pending)
        pending = nxt
        if proj_pieces:
            proj_piece(*proj_pieces.pop(0))
        if kvh == N_KV_HEADS - 1 and (i + 1) in group_ends:
            n_group = len(tiles) // per_block
            rows = slice((i + 1 - n_group) * CHUNK, (i + 1) * CHUNK)
            attn = jnp.concatenate(
                [jnp.concatenate(tiles[b * per_block:(b + 1) * per_block], axis=-1)
                 for b in range(n_group)], axis=0)
            tiles = []
            mixed = jnp.concatenate([outa_ref[rows, :], attn], axis=-1)
            proj_pieces += [(rows, mixed, c) for c in range(o_ref.shape[1] // MXU_DIM)]
    for piece in proj_pieces:
        proj_piece(*piece)


def _mix_out(x, outa, q, k, v, sink, bias_tab, w_out, layer, batch):
    t, d = x.shape
    seq = t // batch
    rows = MIX_OUT_ROWS
    steps = seq // rows
    resident = pl.Buffered(1)
    row_spec = lambda w: pl.BlockSpec((rows, w), lambda b, j: (b * steps + j, 0))
    seq_spec = pl.BlockSpec((seq, KV_W), lambda b, j: (b, 0))
    return pl.pallas_call(
        functools.partial(_mix_out_kernel, layer=layer, n_blocks=seq // CHUNK),
        grid=(batch, steps),
        in_specs=[
            pl.BlockSpec(memory_space=pltpu.SMEM),
            row_spec(d), row_spec(MIX_A), row_spec(MIX_B), seq_spec, seq_spec,
            pl.BlockSpec(bias_tab.shape, lambda b, j: (0, 0, 0, 0), pipeline_mode=resident),
            pl.BlockSpec((MIX_A + MIX_B, d), lambda b, j: (0, 0), pipeline_mode=resident),
        ],
        out_specs=row_spec(d),
        out_shape=jax.ShapeDtypeStruct((t, d), F32),
        scratch_shapes=[
            pltpu.VMEM((seq + 2 * CHUNK, N_VARIANTS * LANES), BF16),
            pltpu.VMEM((seq + 2 * CHUNK, 2 * N_VARIANTS * LANES), BF16),
        ],
        compiler_params=pltpu.CompilerParams(
            dimension_semantics=("parallel", "arbitrary"), vmem_limit_bytes=VMEM_LIMIT_BYTES),
        name="mix_out",
    )(sink, x, outa, q, k, v, bias_tab, w_out)


def _t5_bucket(rel):
    nb = N_BUCKETS // 2
    ret = (rel > 0).astype(np.int32) * nb
    n = np.abs(rel)
    max_exact = nb // 2
    large = max_exact + (np.log(np.maximum(n, 1).astype(np.float32) / max_exact)
                         / math.log(MAX_DISTANCE / max_exact) * (nb - max_exact)).astype(np.int32)
    large = np.minimum(large, nb - 1)
    return ret + np.where(n < max_exact, n, large).astype(np.int32)


def _bias_table(rel_bias):
    rel = np.arange(3 * CHUNK)[None, :] - CHUNK - np.arange(CHUNK)[:, None]
    bucket = jnp.asarray(_t5_bucket(rel), jnp.int32)
    rb = rel_bias.astype(F32)
    bias = jnp.zeros((N_Q_HEADS,) + rel.shape, F32)
    for b in range(N_BUCKETS):
        bias = jnp.where((bucket == b)[None], rb[b][:, None, None], bias)
    band = np.abs(rel) <= CHUNK
    col = np.arange(3 * CHUNK)[None, :]
    edge_masks = [band & (col >= CHUNK), band, band & (col < 2 * CHUNK)]
    allowed = jnp.asarray(sum(m.astype(np.int32) << e for e, m in enumerate(edge_masks)), jnp.int32)
    allowed = jnp.concatenate([allowed, allowed], axis=0)
    head_rows = np.array([[4 * (vi // 2) + (vi % 2), 4 * (vi // 2) + (vi % 2) + 2]
                          for vi in range(N_VARIANTS)])
    per_variant = jnp.stack([jnp.concatenate([bias[a], bias[b]], axis=0) for a, b in head_rows])
    tabs = [jnp.where(((allowed >> e) & 1)[None] == 1, per_variant, NEG_INF)
            for e in range(len(edge_masks))]
    tab = jnp.stack(tabs, axis=0)
    return jnp.concatenate([tab[:, 0::2], tab[:, 1::2]], axis=-1)


def kernel(x, ffn1_norm, ffn1_w_in, ffn1_w_out, mix_norm, w_mix_in, sgu_norm, sgu_w, sgu_b,
           q_norm, k_norm, sink, rel_bias, w_mix_out, ffn2_norm, ffn2_w_in, ffn2_w_out):
    batch, seq, d = x.shape
    depth = ffn1_norm.shape[0]
    assert seq % MIX_OUT_ROWS == 0 and (batch * seq) % FFN_ROWS == 0

    f32 = lambda a: a.astype(F32)
    ffn1_g, mix_g, ffn2_g, sgu_g = f32(ffn1_norm), f32(mix_norm), f32(ffn2_norm), f32(sgu_norm)
    q_g = jnp.tile(f32(q_norm), (1, N_Q_HEADS))
    k_g = jnp.tile(f32(k_norm), (1, N_KV_HEADS))
    sgu_b3 = jnp.broadcast_to(sgu_b.astype(F32)[..., None], sgu_b.shape + (LANES,))
    bias_tab = _bias_table(rel_bias)
    sink = sink.astype(F32)
    wi, wo = ffn1_w_in[0].astype(BF16), ffn1_w_out[0].astype(BF16)

    xt = x.reshape(batch * seq, d)
    for l in range(depth):
        xt, (mix_wi, mix_wo, wi, wo) = _ffn(
            xt, ffn1_g, wi, wo, l, [(w_mix_in, l), (w_mix_out, l), (ffn2_w_in, l), (ffn2_w_out, l)])
        outa, q, k, v = _mix_in(xt, mix_g, mix_wi, sgu_g, sgu_w, sgu_b3, q_g, k_g, l)
        xt = _mix_out(xt, outa, q, k, v, sink, bias_tab, mix_wo, l, batch)
        nxt = [(ffn1_w_in, l + 1), (ffn1_w_out, l + 1)] if l + 1 < depth else []
        xt, nxt_w = _ffn(xt, ffn2_g, wi, wo, l, nxt)
        if nxt:
            wi, wo = nxt_w
    return xt.reshape(batch, seq, d)
```

```python
import functools
import math

import jax
import jax.numpy as jnp
import numpy as np
from jax import lax
from jax.experimental import pallas as pl
from jax.experimental.pallas import tpu as pltpu

F32 = jnp.float32
BF16 = jnp.bfloat16

EPS = 1e-6
NEG_INF = -1e30
LOG2E = math.log2(math.e)

LANES = 128
BF16_SUBLANES = 16
MXU_DIM = 256
VMEM_LIMIT_BYTES = 56 * 1024 * 1024

SGU_GROUPS = 4
CHUNK = 128
N_Q_HEADS = 8
N_KV_HEADS = 2
HEAD_DIM = 64
N_BUCKETS = 32
MAX_DISTANCE = 128
MIX_A = SGU_GROUPS * LANES
MIX_B = N_Q_HEADS * HEAD_DIM
KV_W = N_KV_HEADS * HEAD_DIM
N_VARIANTS = 4

FFN_ROWS = 1024
FFN_CHUNK_TILES = (6, 5)
MIX_IN_ROWS = 1024
MIX_OUT_ROWS = 1024
OUT_PROJ_GROUPS = (2, 2, 2, 2)
FIRST_DOT_ROW_PARTS = 4


def _rms_normed(x, gain):
    w = x.shape[-1]
    inv = lax.rsqrt(jnp.sum(x * x, axis=-1, keepdims=True) + w * EPS)
    return (x * inv) * (gain * math.sqrt(w))


def _row_parts_dot(h, w):
    part = h.shape[0] // FIRST_DOT_ROW_PARTS
    return jnp.concatenate([jnp.dot(h[r0:r0 + part], w, preferred_element_type=F32)
                            for r0 in range(0, h.shape[0], part)], axis=0)


def _ffn_kernel(*refs, d_ff, n_casts, layer):
    x_ref, g_ref, win_ref, wout_ref = refs[:4]
    src_refs = refs[4:4 + n_casts]
    o_ref = refs[4 + n_casts]
    dst_refs = refs[5 + n_casts:]
    for src_ref, dst_ref in zip(src_refs, dst_refs):
        dst_ref[...] = src_ref[...].astype(BF16)
    x = x_ref[...]
    h = _rms_normed(x, g_ref[layer:layer + 1, :]).astype(BF16)
    assert sum(FFN_CHUNK_TILES) * MXU_DIM == d_ff
    y = None
    lo = 0
    for tiles in FFN_CHUNK_TILES:
        tf = tiles * MXU_DIM
        if lo == 0:
            gate = _row_parts_dot(h, win_ref[:, lo:lo + tf])
        else:
            gate = jnp.dot(h, win_ref[:, lo:lo + tf], preferred_element_type=F32)
        up = jnp.dot(h, win_ref[:, d_ff + lo:d_ff + lo + tf], preferred_element_type=F32)
        a = (gate * jax.nn.sigmoid(gate) * up).astype(BF16)
        part = jnp.dot(a, wout_ref[lo:lo + tf, :], preferred_element_type=F32)
        y = part if y is None else y + part
        lo += tf
    o_ref[...] = x + 0.5 * y


def _ffn(x, gain, w_in, w_out, layer, casts=()):
    t, d = x.shape
    d_ff = w_out.shape[0]
    steps = t // FFN_ROWS
    resident = pl.Buffered(1)
    cast_in_specs, cast_out_specs, cast_out_shapes = [], [], []
    for w, l in casts:
        _, rows, cols = w.shape
        slab = rows // steps
        assert slab * steps == rows and slab % BF16_SUBLANES == 0
        cast_in_specs.append(pl.BlockSpec((None, slab, cols), lambda i, l=l: (l, i, 0)))
        cast_out_specs.append(pl.BlockSpec((slab, cols), lambda i: (i, 0)))
        cast_out_shapes.append(jax.ShapeDtypeStruct((rows, cols), BF16))
    outs = pl.pallas_call(
        functools.partial(_ffn_kernel, d_ff=d_ff, n_casts=len(casts), layer=layer),
        grid=(steps,),
        in_specs=[
            pl.BlockSpec((FFN_ROWS, d), lambda i: (i, 0)),
            pl.BlockSpec(gain.shape, lambda i: (0, 0)),
            pl.BlockSpec((d, 2 * d_ff), lambda i: (0, 0), pipeline_mode=resident),
            pl.BlockSpec((d_ff, d), lambda i: (0, 0), pipeline_mode=resident),
        ] + cast_in_specs,
        out_specs=[pl.BlockSpec((FFN_ROWS, d), lambda i: (i, 0))] + cast_out_specs,
        out_shape=[jax.ShapeDtypeStruct((t, d), F32)] + cast_out_shapes,
        compiler_params=pltpu.CompilerParams(
            dimension_semantics=("parallel",), vmem_limit_bytes=VMEM_LIMIT_BYTES),
        name="ffn",
    )(x, gain, w_in, w_out, *[w for w, _ in casts])
    return outs[0], outs[1:]


def _gelu_tanh(x):
    k0 = -2.0 * math.sqrt(2.0 / math.pi) * math.log2(math.e)
    k1 = k0 * 0.044715
    return x / (1.0 + jnp.exp2(x * (k0 + k1 * (x * x))))


def _segment_sumsq(x):
    x2 = x * x
    low = lax.broadcasted_iota(jnp.int32, (x.shape[0], LANES), 1) < HEAD_DIM
    parts = []
    for c in range(0, x.shape[-1], LANES):
        t = x2[:, c:c + LANES]
        s_low = jnp.sum(jnp.where(low, t, 0.0), axis=-1, keepdims=True)
        s_high = jnp.sum(jnp.where(low, 0.0, t), axis=-1, keepdims=True)
        parts.append(jnp.where(low, s_low, s_high))
    return parts[0] if len(parts) == 1 else jnp.concatenate(parts, axis=-1)


def _lane_half_variants(x):
    packed = pltpu.bitcast(x, jnp.uint32)
    swapped = pltpu.bitcast(pltpu.roll(packed, HEAD_DIM, axis=1), x.dtype)
    low = lax.broadcasted_iota(jnp.int32, x.shape, 1) < HEAD_DIM
    zero = jnp.zeros_like(x)
    return [
        jnp.where(low, x, zero),
        jnp.where(low, zero, swapped),
        jnp.where(low, swapped, zero),
        jnp.where(low, zero, x),
    ]


def _mix_in_kernel(x_ref, g_ref, win_ref, sgu_g_ref, sgu_w_ref, sgu_b_ref, qg_ref, kg_ref,
                   outa_ref, q_ref, k_ref, v_ref, *, layer):
    gain_row = lambda ref: ref[layer:layer + 1, :]
    x = x_ref[...]
    rows = x.shape[0]
    h = _rms_normed(x, gain_row(g_ref)).astype(BF16)
    project = lambda lo, hi: jnp.dot(h, win_ref[:, lo:hi], preferred_element_type=F32)
    p_v = project(MIX_A, 2 * MIX_A)
    p_att = project(2 * MIX_A, win_ref.shape[1])
    p_u = project(0, MIX_A)

    zv = _gelu_tanh(p_v)
    mixed = []
    for g in range(SGU_GROUPS):
        v = zv[:, g * LANES:(g + 1) * LANES]
        vn = _rms_normed(v, sgu_g_ref[layer:layer + 1, g * LANES:(g + 1) * LANES]).astype(BF16)
        w_s = sgu_w_ref[g].astype(BF16)
        chunks = [slice(c * CHUNK, (c + 1) * CHUNK) for c in range(rows // CHUNK)]
        mixed.append(jnp.dot(w_s, jnp.concatenate([vn[r] for r in chunks], axis=1),
                             preferred_element_type=F32))

    q = p_att[:, :MIX_B]
    q_inv = lax.rsqrt(_segment_sumsq(q) + HEAD_DIM * EPS)
    q_ref[...] = ((q * q_inv) * (gain_row(qg_ref) * LOG2E)).astype(BF16)

    k = p_att[:, MIX_B:MIX_B + KV_W]
    k_inv = lax.rsqrt(_segment_sumsq(k) + HEAD_DIM * EPS)
    k_ref[...] = ((k * k_inv) * (gain_row(kg_ref) * math.sqrt(HEAD_DIM))).astype(BF16)
    v_ref[...] = p_att[:, MIX_B + KV_W:].astype(BF16)

    zu = _gelu_tanh(p_u)
    for g in range(SGU_GROUPS):
        b_s = sgu_b_ref[g]
        for c, r in enumerate(chunks):
            outa_ref[r, g * LANES:(g + 1) * LANES] = (
                zu[r, g * LANES:(g + 1) * LANES]
                * (mixed[g][:, c * LANES:(c + 1) * LANES] + b_s)).astype(BF16)


def _mix_in(x, gain, w_in, sgu_g, sgu_w, sgu_b, q_g, k_g, layer):
    t, d = x.shape
    in_cols = w_in.shape[-1]
    rows = MIX_IN_ROWS
    row_spec = lambda w: pl.BlockSpec((rows, w), lambda i: (i, 0))
    layer_spec = lambda *shape: pl.BlockSpec((None,) + shape, lambda i: (layer,) + (0,) * len(shape))
    whole = lambda a: pl.BlockSpec(a.shape, lambda i: (0,) * a.ndim)
    return pl.pallas_call(
        functools.partial(_mix_in_kernel, layer=layer),
        grid=(t // rows,),
        in_specs=[
            row_spec(d),
            whole(gain),
            pl.BlockSpec((d, in_cols), lambda i: (0, 0), pipeline_mode=pl.Buffered(1)),
            whole(sgu_g),
            layer_spec(SGU_GROUPS, CHUNK, CHUNK),
            layer_spec(SGU_GROUPS, CHUNK, LANES),
            whole(q_g),
            whole(k_g),
        ],
        out_specs=[row_spec(MIX_A), row_spec(MIX_B), row_spec(KV_W), row_spec(KV_W)],
        out_shape=[
            jax.ShapeDtypeStruct((t, MIX_A), BF16),
            jax.ShapeDtypeStruct((t, MIX_B), BF16),
            jax.ShapeDtypeStruct((t, KV_W), BF16),
            jax.ShapeDtypeStruct((t, KV_W), BF16),
        ],
        compiler_params=pltpu.CompilerParams(
            dimension_semantics=("parallel",), vmem_limit_bytes=VMEM_LIMIT_BYTES),
        name="mix_in",
    )(x, gain, w_in, sgu_g, sgu_w, sgu_b, q_g, k_g)


HALF_W = 3 * CHUNK
KV_COLS = 2 * LANES


def _mix_out_kernel(sink_ref, x_ref, outa_ref, q_ref, k_ref, v_ref, bias_ref, wout_ref,
                    o_ref, kpad_ref, vpad_ref, *, layer, n_blocks):
    j = pl.program_id(1)
    seq = k_ref.shape[0]

    @pl.when(j == 0)
    def _():
        low = lax.broadcasted_iota(jnp.int32, (CHUNK, LANES), 1) < HEAD_DIM
        ones_low = jnp.where(low, 1.0, 0.0).astype(BF16)
        ones_high = jnp.where(low, 0.0, 1.0).astype(BF16)
        kpad_ref[0:CHUNK, :] = jnp.zeros((CHUNK, N_VARIANTS * LANES), BF16)
        kpad_ref[CHUNK + seq:, :] = jnp.zeros((CHUNK, N_VARIANTS * LANES), BF16)
        k_variants = _lane_half_variants(k_ref[...])
        v_variants = _lane_half_variants(v_ref[...])
        for vi in range(N_VARIANTS):
            kpad_ref[CHUNK:CHUNK + seq, vi * LANES:(vi + 1) * LANES] = k_variants[vi]
            c0 = 2 * vi * LANES
            vpad_ref[0:CHUNK, c0:c0 + LANES] = jnp.zeros((CHUNK, LANES), BF16)
            vpad_ref[CHUNK:CHUNK + seq, c0:c0 + LANES] = v_variants[vi]
            vpad_ref[CHUNK + seq:, c0:c0 + LANES] = jnp.zeros((CHUNK, LANES), BF16)
            ones = ones_low if vi % 2 == 0 else ones_high
            for r0 in range(0, seq + 2 * CHUNK, CHUNK):
                vpad_ref[r0:r0 + CHUNK, c0 + LANES:c0 + 2 * LANES] = ones

    blocks_per_step = x_ref.shape[0] // CHUNK
    first_rows = lax.broadcasted_iota(jnp.int32, (2 * CHUNK, 1), 0) < CHUNK
    low_lanes = lax.broadcasted_iota(jnp.int32, (2 * CHUNK, LANES), 1) < HEAD_DIM
    sink_cols = [[jnp.where(first_rows, sink_ref[layer, 4 * kvh + half], sink_ref[layer, 4 * kvh + half + 2])
                  for half in range(2)] for kvh in range(N_KV_HEADS)]

    def scores(i, kvh):
        n = j * blocks_per_step + i
        edge = jnp.where(n == 0, 0, jnp.where(n == n_blocks - 1, 2, 1))
        win = pl.ds(pl.multiple_of(n * CHUNK, CHUNK), HALF_W)
        r = slice(i * CHUNK, (i + 1) * CHUNK)
        cg = 2 * kvh
        lhs = jnp.concatenate([q_ref[r, cg * LANES:(cg + 1) * LANES],
                               q_ref[r, (cg + 1) * LANES:(cg + 2) * LANES]], axis=0)
        keys = jnp.concatenate([kpad_ref[win, kvh * KV_COLS:kvh * KV_COLS + LANES],
                                kpad_ref[win, kvh * KV_COLS + LANES:(kvh + 1) * KV_COLS]], axis=0)
        s = lax.dot_general(lhs, keys, (((1,), (1,)), ((), ())), preferred_element_type=F32)
        s = s + bias_ref[edge, kvh]
        m = [jnp.maximum(jnp.max(s[:, h * HALF_W:(h + 1) * HALF_W], axis=-1, keepdims=True),
                         sink_cols[kvh][h]) for h in range(2)]
        return s, m, win

    def outputs(kvh, s, m, win):
        e = jnp.concatenate([jnp.exp2(s[:, h * HALF_W:(h + 1) * HALF_W] - m[h]) for h in range(2)],
                            axis=-1).astype(BF16)
        vals = jnp.concatenate([vpad_ref[win, 2 * kvh * KV_COLS:(2 * kvh + 1) * KV_COLS],
                                vpad_ref[win, (2 * kvh + 1) * KV_COLS:(2 * kvh + 2) * KV_COLS]], axis=0)
        pv = jnp.dot(e, vals, preferred_element_type=F32)
        e_sink = jnp.where(low_lanes, jnp.exp2(sink_cols[kvh][0] - m[0]),
                           jnp.exp2(sink_cols[kvh][1] - m[1]))
        o_kv = pv[:, :LANES] * (1.0 / (pv[:, LANES:] + e_sink))
        return [o_kv[0:CHUNK].astype(BF16), o_kv[CHUNK:].astype(BF16)]

    work = [(i, kvh) for i in range(blocks_per_step) for kvh in range(N_KV_HEADS)]
    per_block = 2 * N_KV_HEADS
    assert sum(OUT_PROJ_GROUPS) == blocks_per_step
    group_ends = set(np.cumsum(OUT_PROJ_GROUPS).tolist())
    tiles = []
    proj_pieces = []

    def proj_piece(rows, mixed, c):
        cols = slice(c * MXU_DIM, (c + 1) * MXU_DIM)
        o_ref[rows, cols] = x_ref[rows, cols] + jnp.dot(mixed, wout_ref[:, cols],
                                                        preferred_element_type=F32)

    pending = scores(*work[0])
    for idx, (i, kvh) in enumerate(work):
        nxt = scores(*work[idx + 1]) if idx + 1 < len(work) else None
        tiles += outputs(kvh, *pending)
        pending = nxt
        if proj_pieces:
            proj_piece(*proj_pieces.pop(0))
        if kvh == N_KV_HEADS - 1 and (i + 1) in group_ends:
            n_group = len(tiles) // per_block
            rows = slice((i + 1 - n_group) * CHUNK, (i + 1) * CHUNK)
            attn = jnp.concatenate(
                [jnp.concatenate(tiles[b * per_block:(b + 1) * per_block], axis=-1)
                 for b in range(n_group)], axis=0)
            tiles = []
            mixed = jnp.concatenate([outa_ref[rows, :], attn], axis=-1)
            proj_pieces += [(rows, mixed, c) for c in range(o_ref.shape[1] // MXU_DIM)]
    for piece in proj_pieces:
        proj_piece(*piece)


def _mix_out(x, outa, q, k, v, sink, bias_tab, w_out, layer, batch):
    t, d = x.shape
    seq = t // batch
    rows = MIX_OUT_ROWS
    steps = seq // rows
    resident = pl.Buffered(1)
    row_spec = lambda w: pl.BlockSpec((rows, w), lambda b, j: (b * steps + j, 0))
    seq_spec = pl.BlockSpec((seq, KV_W), lambda b, j: (b, 0))
    return pl.pallas_call(
        functools.partial(_mix_out_kernel, layer=layer, n_blocks=seq // CHUNK),
        grid=(batch, steps),
        in_specs=[
            pl.BlockSpec(memory_space=pltpu.SMEM),
            row_spec(d), row_spec(MIX_A), row_spec(MIX_B), seq_spec, seq_spec,
            pl.BlockSpec(bias_tab.shape, lambda b, j: (0, 0, 0, 0), pipeline_mode=resident),
            pl.BlockSpec((MIX_A + MIX_B, d), lambda b, j: (0, 0), pipeline_mode=resident),
        ],
        out_specs=row_spec(d),
        out_shape=jax.ShapeDtypeStruct((t, d), F32),
        scratch_shapes=[
            pltpu.VMEM((seq + 2 * CHUNK, N_VARIANTS * LANES), BF16),
            pltpu.VMEM((seq + 2 * CHUNK, 2 * N_VARIANTS * LANES), BF16),
        ],
        compiler_params=pltpu.CompilerParams(
            dimension_semantics=("parallel", "arbitrary"), vmem_limit_bytes=VMEM_LIMIT_BYTES),
        name="mix_out",
    )(sink, x, outa, q, k, v, bias_tab, w_out)


def _t5_bucket(rel):
    nb = N_BUCKETS // 2
    ret = (rel > 0).astype(np.int32) * nb
    n = np.abs(rel)
    max_exact = nb // 2
    large = max_exact + (np.log(np.maximum(n, 1).astype(np.float32) / max_exact)
                         / math.log(MAX_DISTANCE / max_exact) * (nb - max_exact)).astype(np.int32)
    large = np.minimum(large, nb - 1)
    return ret + np.where(n < max_exact, n, large).astype(np.int32)


def _bias_table(rel_bias):
    rel = np.arange(3 * CHUNK)[None, :] - CHUNK - np.arange(CHUNK)[:, None]
    bucket = lax.optimization_barrier(jnp.asarray(_t5_bucket(rel), jnp.int32))
    rb = rel_bias.astype(F32) * LOG2E
    bias = jnp.zeros((N_Q_HEADS,) + rel.shape, F32)
    for b in range(N_BUCKETS):
        bias = jnp.where((bucket == b)[None], rb[b][:, None, None], bias)
    band = np.abs(rel) <= CHUNK
    col = np.arange(3 * CHUNK)[None, :]
    edge_masks = [band & (col >= CHUNK), band, band & (col < 2 * CHUNK)]
    allowed = lax.optimization_barrier(
        jnp.asarray(sum(m.astype(np.int32) << e for e, m in enumerate(edge_masks)), jnp.int32))
    allowed = jnp.concatenate([allowed, allowed], axis=0)
    head_rows = np.array([[4 * (vi // 2) + (vi % 2), 4 * (vi // 2) + (vi % 2) + 2]
                          for vi in range(N_VARIANTS)])
    per_variant = jnp.stack([jnp.concatenate([bias[a], bias[b]], axis=0) for a, b in head_rows])
    tabs = [jnp.where(((allowed >> e) & 1)[None] == 1, per_variant, NEG_INF)
            for e in range(len(edge_masks))]
    tab = jnp.stack(tabs, axis=0)
    return jnp.concatenate([tab[:, 0::2], tab[:, 1::2]], axis=-1)


def kernel(x, ffn1_norm, ffn1_w_in, ffn1_w_out, mix_norm, w_mix_in, sgu_norm, sgu_w, sgu_b,
           q_norm, k_norm, sink, rel_bias, w_mix_out, ffn2_norm, ffn2_w_in, ffn2_w_out):
    batch, seq, d = x.shape
    depth = ffn1_norm.shape[0]
    assert seq % MIX_OUT_ROWS == 0 and (batch * seq) % FFN_ROWS == 0

    f32 = lambda a: a.astype(F32)
    ffn1_g, mix_g, ffn2_g, sgu_g = f32(ffn1_norm), f32(mix_norm), f32(ffn2_norm), f32(sgu_norm)
    q_g = jnp.tile(f32(q_norm), (1, N_Q_HEADS))
    k_g = jnp.tile(f32(k_norm), (1, N_KV_HEADS))
    sgu_b3 = jnp.broadcast_to(sgu_b.astype(F32)[..., None], sgu_b.shape + (LANES,))
    bias_tab = _bias_table(rel_bias)
    sink = sink.astype(F32) * LOG2E
    wi, wo = ffn1_w_in[0].astype(BF16), ffn1_w_out[0].astype(BF16)

    xt = x.reshape(batch * seq, d)
    for l in range(depth):
        xt, (mix_wi, mix_wo, wi, wo) = _ffn(
            xt, ffn1_g, wi, wo, l, [(w_mix_in, l), (w_mix_out, l), (ffn2_w_in, l), (ffn2_w_out, l)])
        outa, q, k, v = _mix_in(xt, mix_g, mix_wi, sgu_g, sgu_w, sgu_b3, q_g, k_g, l)
        xt = _mix_out(xt, outa, q, k, v, sink, bias_tab, mix_wo, l, batch)
        nxt = [(ffn1_w_in, l + 1), (ffn1_w_out, l + 1)] if l + 1 < depth else []
        xt, nxt_w = _ffn(xt, ffn2_g, wi, wo, l, nxt)
        if nxt:
            wi, wo = nxt_w
    return xt.reshape(batch, seq, d)
```

```python
import functools
import math

import jax
import jax.numpy as jnp
import numpy as np
from jax import lax
from jax.experimental import pallas as pl
from jax.experimental.pallas import tpu as pltpu

F32 = jnp.float32
BF16 = jnp.bfloat16

EPS = 1e-6
NEG_INF = -1e30
LOG2E = math.log2(math.e)

LANES = 128
BF16_SUBLANES = 16
MXU_DIM = 256
VMEM_LIMIT_BYTES = 56 * 1024 * 1024

SGU_GROUPS = 4
CHUNK = 128
N_Q_HEADS = 8
N_KV_HEADS = 2
HEAD_DIM = 64
N_BUCKETS = 32
MAX_DISTANCE = 128
MIX_A = SGU_GROUPS * LANES
MIX_B = N_Q_HEADS * HEAD_DIM
KV_W = N_KV_HEADS * HEAD_DIM
N_VARIANTS = 4

FFN_ROWS = 1024
FFN_CHUNK_TILES = (6, 5)
MIX_IN_ROWS = 1024
MIX_OUT_ROWS = 1024
OUT_PROJ_GROUPS = (2, 2, 2, 2)
FIRST_DOT_ROW_PARTS = 4


def _rms_normed(x, gain):
    w = x.shape[-1]
    inv = lax.rsqrt(jnp.sum(x * x, axis=-1, keepdims=True) + w * EPS)
    return (x * inv) * (gain * math.sqrt(w))


def _row_parts_dot(h, w):
    part = h.shape[0] // FIRST_DOT_ROW_PARTS
    return jnp.concatenate([jnp.dot(h[r0:r0 + part], w, preferred_element_type=F32)
                            for r0 in range(0, h.shape[0], part)], axis=0)


def _ffn_kernel(*refs, d_ff, n_casts, layer):
    x_ref, g_ref, win_ref, wout_ref = refs[:4]
    src_refs = refs[4:4 + n_casts]
    o_ref = refs[4 + n_casts]
    dst_refs = refs[5 + n_casts:]
    for src_ref, dst_ref in zip(src_refs, dst_refs):
        dst_ref[...] = src_ref[...].astype(BF16)
    x = x_ref[...]
    h = _rms_normed(x, g_ref[layer:layer + 1, :]).astype(BF16)
    assert sum(FFN_CHUNK_TILES) * MXU_DIM == d_ff
    y = None
    lo = 0
    for tiles in FFN_CHUNK_TILES:
        tf = tiles * MXU_DIM
        if lo == 0:
            gate = _row_parts_dot(h, win_ref[:, lo:lo + tf])
        else:
            gate = jnp.dot(h, win_ref[:, lo:lo + tf], preferred_element_type=F32)
        up = jnp.dot(h, win_ref[:, d_ff + lo:d_ff + lo + tf], preferred_element_type=F32)
        a = (gate * jax.nn.sigmoid(gate) * up).astype(BF16)
        part = jnp.dot(a, wout_ref[lo:lo + tf, :], preferred_element_type=F32)
        y = part if y is None else y + part
        lo += tf
    o_ref[...] = x + 0.5 * y


def _ffn(x, gain, w_in, w_out, layer, casts=()):
    t, d = x.shape
    d_ff = w_out.shape[0]
    steps = t // FFN_ROWS
    resident = pl.Buffered(1)
    cast_in_specs, cast_out_specs, cast_out_shapes = [], [], []
    for w, l in casts:
        _, rows, cols = w.shape
        slab = rows // steps
        assert slab * steps == rows and slab % BF16_SUBLANES == 0
        cast_in_specs.append(pl.BlockSpec((None, slab, cols), lambda i, l=l: (l, i, 0)))
        cast_out_specs.append(pl.BlockSpec((slab, cols), lambda i: (i, 0)))
        cast_out_shapes.append(jax.ShapeDtypeStruct((rows, cols), BF16))
    outs = pl.pallas_call(
        functools.partial(_ffn_kernel, d_ff=d_ff, n_casts=len(casts), layer=layer),
        grid=(steps,),
        in_specs=[
            pl.BlockSpec((FFN_ROWS, d), lambda i: (i, 0)),
            pl.BlockSpec(gain.shape, lambda i: (0, 0)),
            pl.BlockSpec((d, 2 * d_ff), lambda i: (0, 0), pipeline_mode=resident),
            pl.BlockSpec((d_ff, d), lambda i: (0, 0), pipeline_mode=resident),
        ] + cast_in_specs,
        out_specs=[pl.BlockSpec((FFN_ROWS, d), lambda i: (i, 0))] + cast_out_specs,
        out_shape=[jax.ShapeDtypeStruct((t, d), F32)] + cast_out_shapes,
        compiler_params=pltpu.CompilerParams(
            dimension_semantics=("parallel",), vmem_limit_bytes=VMEM_LIMIT_BYTES),
        name="ffn",
    )(x, gain, w_in, w_out, *[w for w, _ in casts])
    return outs[0], outs[1:]


def _gelu_tanh(x):
    k0 = -2.0 * math.sqrt(2.0 / math.pi) * math.log2(math.e)
    k1 = k0 * 0.044715
    return x / (1.0 + jnp.exp2(x * (k0 + k1 * (x * x))))


def _segment_sumsq(x):
    x2 = x * x
    low = lax.broadcasted_iota(jnp.int32, (x.shape[0], LANES), 1) < HEAD_DIM
    parts = []
    for c in range(0, x.shape[-1], LANES):
        t = x2[:, c:c + LANES]
        s_low = jnp.sum(jnp.where(low, t, 0.0), axis=-1, keepdims=True)
        s_high = jnp.sum(jnp.where(low, 0.0, t), axis=-1, keepdims=True)
        parts.append(jnp.where(low, s_low, s_high))
    return parts[0] if len(parts) == 1 else jnp.concatenate(parts, axis=-1)


def _lane_half_variants(x):
    packed = pltpu.bitcast(x, jnp.uint32)
    swapped = pltpu.bitcast(pltpu.roll(packed, HEAD_DIM, axis=1), x.dtype)
    low = lax.broadcasted_iota(jnp.int32, x.shape, 1) < HEAD_DIM
    zero = jnp.zeros_like(x)
    return [
        jnp.where(low, x, zero),
        jnp.where(low, zero, swapped),
        jnp.where(low, swapped, zero),
        jnp.where(low, zero, x),
    ]


def _mix_in_kernel(x_ref, g_ref, win_ref, sgu_g_ref, sgu_w_ref, sgu_b_ref, qg_ref, kg_ref,
                   outa_ref, q_ref, k_ref, v_ref, *, layer):
    gain_row = lambda ref: ref[layer:layer + 1, :]
    x = x_ref[...]
    rows = x.shape[0]
    h = _rms_normed(x, gain_row(g_ref)).astype(BF16)
    project = lambda lo, hi: jnp.dot(h, win_ref[:, lo:hi], preferred_element_type=F32)
    p_v = project(MIX_A, 2 * MIX_A)
    p_att = project(2 * MIX_A, win_ref.shape[1])
    p_u = project(0, MIX_A)

    zv = _gelu_tanh(p_v)
    mixed = []
    for g in range(SGU_GROUPS):
        v = zv[:, g * LANES:(g + 1) * LANES]
        vn = _rms_normed(v, sgu_g_ref[layer:layer + 1, g * LANES:(g + 1) * LANES]).astype(BF16)
        w_s = sgu_w_ref[g].astype(BF16)
        chunks = [slice(c * CHUNK, (c + 1) * CHUNK) for c in range(rows // CHUNK)]
        mixed.append(jnp.dot(w_s, jnp.concatenate([vn[r] for r in chunks], axis=1),
                             preferred_element_type=F32))

    q = p_att[:, :MIX_B]
    q_inv = lax.rsqrt(_segment_sumsq(q) + HEAD_DIM * EPS)
    q_ref[...] = ((q * q_inv) * (gain_row(qg_ref) * LOG2E)).astype(BF16)

    k = p_att[:, MIX_B:MIX_B + KV_W]
    k_inv = lax.rsqrt(_segment_sumsq(k) + HEAD_DIM * EPS)
    k_ref[...] = ((k * k_inv) * (gain_row(kg_ref) * math.sqrt(HEAD_DIM))).astype(BF16)
    v_ref[...] = p_att[:, MIX_B + KV_W:].astype(BF16)

    zu = _gelu_tanh(p_u)
    for g in range(SGU_GROUPS):
        b_s = sgu_b_ref[g]
        for c, r in enumerate(chunks):
            outa_ref[r, g * LANES:(g + 1) * LANES] = (
                zu[r, g * LANES:(g + 1) * LANES]
                * (mixed[g][:, c * LANES:(c + 1) * LANES] + b_s)).astype(BF16)


def _mix_in(x, gain, w_in, sgu_g, sgu_w, sgu_b, q_g, k_g, layer):
    t, d = x.shape
    in_cols = w_in.shape[-1]
    rows = MIX_IN_ROWS
    row_spec = lambda w: pl.BlockSpec((rows, w), lambda i: (i, 0))
    layer_spec = lambda *shape: pl.BlockSpec((None,) + shape, lambda i: (layer,) + (0,) * len(shape))
    whole = lambda a: pl.BlockSpec(a.shape, lambda i: (0,) * a.ndim)
    return pl.pallas_call(
        functools.partial(_mix_in_kernel, layer=layer),
        grid=(t // rows,),
        in_specs=[
            row_spec(d),
            whole(gain),
            pl.BlockSpec((d, in_cols), lambda i: (0, 0), pipeline_mode=pl.Buffered(1)),
            whole(sgu_g),
            layer_spec(SGU_GROUPS, CHUNK, CHUNK),
            layer_spec(SGU_GROUPS, CHUNK, LANES),
            whole(q_g),
            whole(k_g),
        ],
        out_specs=[row_spec(MIX_A), row_spec(MIX_B), row_spec(KV_W), row_spec(KV_W)],
        out_shape=[
            jax.ShapeDtypeStruct((t, MIX_A), BF16),
            jax.ShapeDtypeStruct((t, MIX_B), BF16),
            jax.ShapeDtypeStruct((t, KV_W), BF16),
            jax.ShapeDtypeStruct((t, KV_W), BF16),
        ],
        compiler_params=pltpu.CompilerParams(
            dimension_semantics=("parallel",), vmem_limit_bytes=VMEM_LIMIT_BYTES),
        name="mix_in",
    )(x, gain, w_in, sgu_g, sgu_w, sgu_b, q_g, k_g)


HALF_W = 3 * CHUNK
KV_COLS = 2 * LANES


def _mix_out_kernel(sink_ref, x_ref, outa_ref, q_ref, k_ref, v_ref, bias_ref, wout_ref,
                    o_ref, kpad_ref, vpad_ref, *, layer, n_blocks):
    j = pl.program_id(1)
    seq = k_ref.shape[0]

    @pl.when(j == 0)
    def _():
        low = lax.broadcasted_iota(jnp.int32, (CHUNK, LANES), 1) < HEAD_DIM
        ones_low = jnp.where(low, 1.0, 0.0).astype(BF16)
        ones_high = jnp.where(low, 0.0, 1.0).astype(BF16)
        kpad_ref[0:CHUNK, :] = jnp.zeros((CHUNK, N_VARIANTS * LANES), BF16)
        kpad_ref[CHUNK + seq:, :] = jnp.zeros((CHUNK, N_VARIANTS * LANES), BF16)
        k_variants = _lane_half_variants(k_ref[...])
        v_variants = _lane_half_variants(v_ref[...])
        for vi in range(N_VARIANTS):
            kpad_ref[CHUNK:CHUNK + seq, vi * LANES:(vi + 1) * LANES] = k_variants[vi]
            c0 = 2 * vi * LANES
            vpad_ref[0:CHUNK, c0:c0 + LANES] = jnp.zeros((CHUNK, LANES), BF16)
            vpad_ref[CHUNK:CHUNK + seq, c0:c0 + LANES] = v_variants[vi]
            vpad_ref[CHUNK + seq:, c0:c0 + LANES] = jnp.zeros((CHUNK, LANES), BF16)
            ones = ones_low if vi % 2 == 0 else ones_high
            for r0 in range(0, seq + 2 * CHUNK, CHUNK):
                vpad_ref[r0:r0 + CHUNK, c0 + LANES:c0 + 2 * LANES] = ones

    blocks_per_step = x_ref.shape[0] // CHUNK
    first_rows = lax.broadcasted_iota(jnp.int32, (2 * CHUNK, 1), 0) < CHUNK
    low_lanes = lax.broadcasted_iota(jnp.int32, (2 * CHUNK, LANES), 1) < HEAD_DIM
    sink_cols = [[jnp.where(first_rows, sink_ref[layer, 4 * kvh + half], sink_ref[layer, 4 * kvh + half + 2])
                  for half in range(2)] for kvh in range(N_KV_HEADS)]
    sink_lanes = [jnp.where(low_lanes, cols[0], cols[1]) for cols in sink_cols]

    def scores(i, kvh):
        n = j * blocks_per_step + i
        edge = jnp.where(n == 0, 0, jnp.where(n == n_blocks - 1, 2, 1))
        win = pl.ds(pl.multiple_of(n * CHUNK, CHUNK), HALF_W)
        r = slice(i * CHUNK, (i + 1) * CHUNK)
        cg = 2 * kvh
        lhs = jnp.concatenate([q_ref[r, cg * LANES:(cg + 1) * LANES],
                               q_ref[r, (cg + 1) * LANES:(cg + 2) * LANES]], axis=0)
        keys = jnp.concatenate([kpad_ref[win, kvh * KV_COLS:kvh * KV_COLS + LANES],
                                kpad_ref[win, kvh * KV_COLS + LANES:(kvh + 1) * KV_COLS]], axis=0)
        s = lax.dot_general(lhs, keys, (((1,), (1,)), ((), ())), preferred_element_type=F32)
        s = s + bias_ref[edge, kvh]
        m = [jnp.maximum(jnp.max(s[:, h * HALF_W:(h + 1) * HALF_W], axis=-1, keepdims=True),
                         sink_cols[kvh][h]) for h in range(2)]
        return s, m, win

    def outputs(kvh, s, m, win):
        e = jnp.concatenate([jnp.exp2(s[:, h * HALF_W:(h + 1) * HALF_W] - m[h]) for h in range(2)],
                            axis=-1).astype(BF16)
        vals = jnp.concatenate([vpad_ref[win, 2 * kvh * KV_COLS:(2 * kvh + 1) * KV_COLS],
                                vpad_ref[win, (2 * kvh + 1) * KV_COLS:(2 * kvh + 2) * KV_COLS]], axis=0)
        pv = jnp.dot(e, vals, preferred_element_type=F32)
        e_sink = jnp.exp2(sink_lanes[kvh] - jnp.where(low_lanes, m[0], m[1]))
        o_kv = pv[:, :LANES] * (1.0 / (pv[:, LANES:] + e_sink))
        return [o_kv[0:CHUNK].astype(BF16), o_kv[CHUNK:].astype(BF16)]

    work = [(i, kvh) for i in range(blocks_per_step) for kvh in range(N_KV_HEADS)]
    per_block = 2 * N_KV_HEADS
    assert sum(OUT_PROJ_GROUPS) == blocks_per_step
    group_ends = set(np.cumsum(OUT_PROJ_GROUPS).tolist())
    tiles = []
    proj_pieces = []

    def proj_piece(rows, mixed, c):
        cols = slice(c * MXU_DIM, (c + 1) * MXU_DIM)
        o_ref[rows, cols] = x_ref[rows, cols] + jnp.dot(mixed, wout_ref[:, cols],
                                                        preferred_element_type=F32)

    pending = scores(*work[0])
    for idx, (i, kvh) in enumerate(work):
        nxt = scores(*work[idx + 1]) if idx + 1 < len(work) else None
        tiles += outputs(kvh, *pending)
        pending = nxt
        if proj_pieces:
            proj_piece(*proj_pieces.pop(0))
        if kvh == N_KV_HEADS - 1 and (i + 1) in group_ends:
            n_group = len(tiles) // per_block
            rows = slice((i + 1 - n_group) * CHUNK, (i + 1) * CHUNK)
            attn = jnp.concatenate(
                [jnp.concatenate(tiles[b * per_block:(b + 1) * per_block], axis=-1)
                 for b in range(n_group)], axis=0)
            tiles = []
            mixed = jnp.concatenate([outa_ref[rows, :], attn], axis=-1)
            proj_pieces += [(rows, mixed, c) for c in range(o_ref.shape[1] // MXU_DIM)]
    for piece in proj_pieces:
        proj_piece(*piece)


def _mix_out(x, outa, q, k, v, sink, bias_tab, w_out, layer, batch):
    t, d = x.shape
    seq = t // batch
    rows = MIX_OUT_ROWS
    steps = seq // rows
    resident = pl.Buffered(1)
    row_spec = lambda w: pl.BlockSpec((rows, w), lambda b, j: (b * steps + j, 0))
    seq_spec = pl.BlockSpec((seq, KV_W), lambda b, j: (b, 0))
    return pl.pallas_call(
        functools.partial(_mix_out_kernel, layer=layer, n_blocks=seq // CHUNK),
        grid=(batch, steps),
        in_specs=[
            pl.BlockSpec(memory_space=pltpu.SMEM),
            row_spec(d), row_spec(MIX_A), row_spec(MIX_B), seq_spec, seq_spec,
            pl.BlockSpec(bias_tab.shape, lambda b, j: (0, 0, 0, 0), pipeline_mode=resident),
            pl.BlockSpec((MIX_A + MIX_B, d), lambda b, j: (0, 0), pipeline_mode=resident),
        ],
        out_specs=row_spec(d),
        out_shape=jax.ShapeDtypeStruct((t, d), F32),
        scratch_shapes=[
            pltpu.VMEM((seq + 2 * CHUNK, N_VARIANTS * LANES), BF16),
            pltpu.VMEM((seq + 2 * CHUNK, 2 * N_VARIANTS * LANES), BF16),
        ],
        compiler_params=pltpu.CompilerParams(
            dimension_semantics=("parallel", "arbitrary"), vmem_limit_bytes=VMEM_LIMIT_BYTES),
        name="mix_out",
    )(sink, x, outa, q, k, v, bias_tab, w_out)


def _t5_bucket(rel):
    nb = N_BUCKETS // 2
    ret = (rel > 0).astype(np.int32) * nb
    n = np.abs(rel)
    max_exact = nb // 2
    large = max_exact + (np.log(np.maximum(n, 1).astype(np.float32) / max_exact)
                         / math.log(MAX_DISTANCE / max_exact) * (nb - max_exact)).astype(np.int32)
    large = np.minimum(large, nb - 1)
    return ret + np.where(n < max_exact, n, large).astype(np.int32)


def _bias_table(rel_bias):
    rel = np.arange(3 * CHUNK)[None, :] - CHUNK - np.arange(CHUNK)[:, None]
    bucket = lax.optimization_barrier(jnp.asarray(_t5_bucket(rel), jnp.int32))
    rb = rel_bias.astype(F32) * LOG2E
    bias = jnp.zeros((N_Q_HEADS,) + rel.shape, F32)
    for b in range(N_BUCKETS):
        bias = jnp.where((bucket == b)[None], rb[b][:, None, None], bias)
    band = np.abs(rel) <= CHUNK
    col = np.arange(3 * CHUNK)[None, :]
    edge_masks = [band & (col >= CHUNK), band, band & (col < 2 * CHUNK)]
    allowed = lax.optimization_barrier(
        jnp.asarray(sum(m.astype(np.int32) << e for e, m in enumerate(edge_masks)), jnp.int32))
    allowed = jnp.concatenate([allowed, allowed], axis=0)
    head_rows = np.array([[4 * (vi // 2) + (vi % 2), 4 * (vi // 2) + (vi % 2) + 2]
                          for vi in range(N_VARIANTS)])
    per_variant = jnp.stack([jnp.concatenate([bias[a], bias[b]], axis=0) for a, b in head_rows])
    tabs = [jnp.where(((allowed >> e) & 1)[None] == 1, per_variant, NEG_INF)
            for e in range(len(edge_masks))]
    tab = jnp.stack(tabs, axis=0)
    return jnp.concatenate([tab[:, 0::2], tab[:, 1::2]], axis=-1)


def kernel(x, ffn1_norm, ffn1_w_in, ffn1_w_out, mix_norm, w_mix_in, sgu_norm, sgu_w, sgu_b,
           q_norm, k_norm, sink, rel_bias, w_mix_out, ffn2_norm, ffn2_w_in, ffn2_w_out):
    batch, seq, d = x.shape
    depth = ffn1_norm.shape[0]
    assert seq % MIX_OUT_ROWS == 0 and (batch * seq) % FFN_ROWS == 0

    f32 = lambda a: a.astype(F32)
    ffn1_g, mix_g, ffn2_g, sgu_g = f32(ffn1_norm), f32(mix_norm), f32(ffn2_norm), f32(sgu_norm)
    q_g = jnp.tile(f32(q_norm), (1, N_Q_HEADS))
    k_g = jnp.tile(f32(k_norm), (1, N_KV_HEADS))
    sgu_b3 = jnp.broadcast_to(sgu_b.astype(F32)[..., None], sgu_b.shape + (LANES,))
    bias_tab = _bias_table(rel_bias)
    sink = sink.astype(F32) * LOG2E
    wi, wo = ffn1_w_in[0].astype(BF16), ffn1_w_out[0].astype(BF16)

    xt = x.reshape(batch * seq, d)
    for l in range(depth):
        xt, (mix_wi, mix_wo, wi, wo) = _ffn(
            xt, ffn1_g, wi, wo, l, [(w_mix_in, l), (w_mix_out, l), (ffn2_w_in, l), (ffn2_w_out, l)])
        outa, q, k, v = _mix_in(xt, mix_g, mix_wi, sgu_g, sgu_w, sgu_b3, q_g, k_g, l)
        xt = _mix_out(xt, outa, q, k, v, sink, bias_tab, mix_wo, l, batch)
        nxt = [(ffn1_w_in, l + 1), (ffn1_w_out, l + 1)] if l + 1 < depth else []
        xt, nxt_w = _ffn(xt, ffn2_g, wi, wo, l, nxt)
        if nxt:
            wi, wo = nxt_w
    return xt.reshape(batch, seq, d)
```

```python
import functools
import math

import jax
import jax.numpy as jnp
import numpy as np
from jax import lax
from jax.experimental import pallas as pl
from jax.experimental.pallas import tpu as pltpu

F32 = jnp.float32
BF16 = jnp.bfloat16

EPS = 1e-6
NEG_INF = -1e30
LOG2E = math.log2(math.e)

LANES = 128
BF16_SUBLANES = 16
MXU_DIM = 256
VMEM_LIMIT_BYTES = 56 * 1024 * 1024

SGU_GROUPS = 4
CHUNK = 128
N_Q_HEADS = 8
N_KV_HEADS = 2
HEAD_DIM = 64
N_BUCKETS = 32
MAX_DISTANCE = 128
MIX_A = SGU_GROUPS * LANES
MIX_B = N_Q_HEADS * HEAD_DIM
KV_W = N_KV_HEADS * HEAD_DIM
N_VARIANTS = 4

FFN_ROWS = 1024
FFN_CHUNK_TILES = (6, 5)
MIX_IN_ROWS = 1024
MIX_OUT_ROWS = 1024
OUT_PROJ_GROUPS = (2, 2, 2, 2)
FIRST_DOT_ROW_PARTS = 4


def _rms_normed(x, gain):
    w = x.shape[-1]
    inv = lax.rsqrt(jnp.sum(x * x, axis=-1, keepdims=True) + w * EPS)
    return (x * inv) * (gain * math.sqrt(w))


def _row_parts_dot(h, w):
    part = h.shape[0] // FIRST_DOT_ROW_PARTS
    return jnp.concatenate([jnp.dot(h[r0:r0 + part], w, preferred_element_type=F32)
                            for r0 in range(0, h.shape[0], part)], axis=0)


def _ffn_kernel(*refs, d_ff, n_casts, layer, has_delta):
    x_ref = refs[0]
    delta_ref = refs[1] if has_delta else None
    refs = refs[1 + has_delta:]
    g_ref, win_ref, wout_ref = refs[:3]
    src_refs = refs[3:3 + n_casts]
    o_ref = refs[3 + n_casts]
    dst_refs = refs[4 + n_casts:]
    for src_ref, dst_ref in zip(src_refs, dst_refs):
        dst_ref[...] = src_ref[...].astype(BF16)
    x = x_ref[...] + delta_ref[...] if has_delta else x_ref[...]
    h = _rms_normed(x, g_ref[layer:layer + 1, :]).astype(BF16)
    assert sum(FFN_CHUNK_TILES) * MXU_DIM == d_ff
    y = None
    lo = 0
    for tiles in FFN_CHUNK_TILES:
        tf = tiles * MXU_DIM
        if lo == 0:
            gate = _row_parts_dot(h, win_ref[:, lo:lo + tf])
        else:
            gate = jnp.dot(h, win_ref[:, lo:lo + tf], preferred_element_type=F32)
        up = jnp.dot(h, win_ref[:, d_ff + lo:d_ff + lo + tf], preferred_element_type=F32)
        a = (gate * jax.nn.sigmoid(gate) * up).astype(BF16)
        part = jnp.dot(a, wout_ref[lo:lo + tf, :], preferred_element_type=F32)
        y = part if y is None else y + part
        lo += tf
    o_ref[...] = x + 0.5 * y


def _ffn(x, gain, w_in, w_out, layer, casts=(), delta=None):
    t, d = x.shape
    d_ff = w_out.shape[0]
    steps = t // FFN_ROWS
    resident = pl.Buffered(1)
    cast_in_specs, cast_out_specs, cast_out_shapes = [], [], []
    for w, l in casts:
        _, rows, cols = w.shape
        slab = rows // steps
        assert slab * steps == rows and slab % BF16_SUBLANES == 0
        cast_in_specs.append(pl.BlockSpec((None, slab, cols), lambda i, l=l: (l, i, 0)))
        cast_out_specs.append(pl.BlockSpec((slab, cols), lambda i: (i, 0)))
        cast_out_shapes.append(jax.ShapeDtypeStruct((rows, cols), BF16))
    row_spec = pl.BlockSpec((FFN_ROWS, d), lambda i: (i, 0))
    streams = [x] if delta is None else [x, delta]
    outs = pl.pallas_call(
        functools.partial(_ffn_kernel, d_ff=d_ff, n_casts=len(casts), layer=layer,
                          has_delta=delta is not None),
        grid=(steps,),
        in_specs=[row_spec] * len(streams) + [
            pl.BlockSpec(gain.shape, lambda i: (0, 0)),
            pl.BlockSpec((d, 2 * d_ff), lambda i: (0, 0), pipeline_mode=resident),
            pl.BlockSpec((d_ff, d), lambda i: (0, 0), pipeline_mode=resident),
        ] + cast_in_specs,
        out_specs=[pl.BlockSpec((FFN_ROWS, d), lambda i: (i, 0))] + cast_out_specs,
        out_shape=[jax.ShapeDtypeStruct((t, d), F32)] + cast_out_shapes,
        compiler_params=pltpu.CompilerParams(
            dimension_semantics=("parallel",), vmem_limit_bytes=VMEM_LIMIT_BYTES),
        name="ffn",
    )(*streams, gain, w_in, w_out, *[w for w, _ in casts])
    return outs[0], outs[1:]


def _gelu_tanh(x):
    k0 = -2.0 * math.sqrt(2.0 / math.pi) * math.log2(math.e)
    k1 = k0 * 0.044715
    return x / (1.0 + jnp.exp2(x * (k0 + k1 * (x * x))))


def _segment_sumsq(x):
    x2 = x * x
    low = lax.broadcasted_iota(jnp.int32, (x.shape[0], LANES), 1) < HEAD_DIM
    parts = []
    for c in range(0, x.shape[-1], LANES):
        t = x2[:, c:c + LANES]
        s_low = jnp.sum(jnp.where(low, t, 0.0), axis=-1, keepdims=True)
        s_high = jnp.sum(jnp.where(low, 0.0, t), axis=-1, keepdims=True)
        parts.append(jnp.where(low, s_low, s_high))
    return parts[0] if len(parts) == 1 else jnp.concatenate(parts, axis=-1)


def _lane_half_variants(x):
    packed = pltpu.bitcast(x, jnp.uint32)
    swapped = pltpu.bitcast(pltpu.roll(packed, HEAD_DIM, axis=1), x.dtype)
    low = lax.broadcasted_iota(jnp.int32, x.shape, 1) < HEAD_DIM
    zero = jnp.zeros_like(x)
    return [
        jnp.where(low, x, zero),
        jnp.where(low, zero, swapped),
        jnp.where(low, swapped, zero),
        jnp.where(low, zero, x),
    ]


def _mix_in_kernel(x_ref, g_ref, win_ref, sgu_g_ref, sgu_w_ref, sgu_b_ref, qg_ref, kg_ref,
                   outa_ref, q_ref, k_ref, v_ref, *, layer):
    gain_row = lambda ref: ref[layer:layer + 1, :]
    x = x_ref[...]
    rows = x.shape[0]
    h = _rms_normed(x, gain_row(g_ref)).astype(BF16)
    project = lambda lo, hi: jnp.dot(h, win_ref[:, lo:hi], preferred_element_type=F32)
    p_v = project(MIX_A, 2 * MIX_A)
    p_att = project(2 * MIX_A, win_ref.shape[1])
    p_u = project(0, MIX_A)

    zv = _gelu_tanh(p_v)
    mixed = []
    for g in range(SGU_GROUPS):
        v = zv[:, g * LANES:(g + 1) * LANES]
        vn = _rms_normed(v, sgu_g_ref[layer:layer + 1, g * LANES:(g + 1) * LANES]).astype(BF16)
        w_s = sgu_w_ref[g].astype(BF16)
        chunks = [slice(c * CHUNK, (c + 1) * CHUNK) for c in range(rows // CHUNK)]
        mixed.append(jnp.dot(w_s, jnp.concatenate([vn[r] for r in chunks], axis=1),
                             preferred_element_type=F32))

    q = p_att[:, :MIX_B]
    q_inv = lax.rsqrt(_segment_sumsq(q) + HEAD_DIM * EPS)
    q_ref[...] = ((q * q_inv) * (gain_row(qg_ref) * LOG2E)).astype(BF16)

    k = p_att[:, MIX_B:MIX_B + KV_W]
    k_inv = lax.rsqrt(_segment_sumsq(k) + HEAD_DIM * EPS)
    k_ref[...] = ((k * k_inv) * (gain_row(kg_ref) * math.sqrt(HEAD_DIM))).astype(BF16)
    v_ref[...] = p_att[:, MIX_B + KV_W:].astype(BF16)

    zu = _gelu_tanh(p_u)
    for g in range(SGU_GROUPS):
        b_s = sgu_b_ref[g]
        for c, r in enumerate(chunks):
            outa_ref[r, g * LANES:(g + 1) * LANES] = (
                zu[r, g * LANES:(g + 1) * LANES]
                * (mixed[g][:, c * LANES:(c + 1) * LANES] + b_s)).astype(BF16)


def _mix_in(x, gain, w_in, sgu_g, sgu_w, sgu_b, q_g, k_g, layer):
    t, d = x.shape
    in_cols = w_in.shape[-1]
    rows = MIX_IN_ROWS
    row_spec = lambda w: pl.BlockSpec((rows, w), lambda i: (i, 0))
    layer_spec = lambda *shape: pl.BlockSpec((None,) + shape, lambda i: (layer,) + (0,) * len(shape))
    whole = lambda a: pl.BlockSpec(a.shape, lambda i: (0,) * a.ndim)
    return pl.pallas_call(
        functools.partial(_mix_in_kernel, layer=layer),
        grid=(t // rows,),
        in_specs=[
            row_spec(d),
            whole(gain),
            pl.BlockSpec((d, in_cols), lambda i: (0, 0), pipeline_mode=pl.Buffered(1)),
            whole(sgu_g),
            layer_spec(SGU_GROUPS, CHUNK, CHUNK),
            layer_spec(SGU_GROUPS, CHUNK, LANES),
            whole(q_g),
            whole(k_g),
        ],
        out_specs=[row_spec(MIX_A), row_spec(MIX_B), row_spec(KV_W), row_spec(KV_W)],
        out_shape=[
            jax.ShapeDtypeStruct((t, MIX_A), BF16),
            jax.ShapeDtypeStruct((t, MIX_B), BF16),
            jax.ShapeDtypeStruct((t, KV_W), BF16),
            jax.ShapeDtypeStruct((t, KV_W), BF16),
        ],
        compiler_params=pltpu.CompilerParams(
            dimension_semantics=("parallel",), vmem_limit_bytes=VMEM_LIMIT_BYTES),
        name="mix_in",
    )(x, gain, w_in, sgu_g, sgu_w, sgu_b, q_g, k_g)


HALF_W = 3 * CHUNK
KV_COLS = 2 * LANES


def _mix_out_kernel(sink_ref, outa_ref, q_ref, k_ref, v_ref, bias_ref, wout_ref, *rest,
                    layer, n_blocks, n_casts):
    src_refs = rest[:n_casts]
    o_ref = rest[n_casts]
    dst_refs = rest[n_casts + 1:2 * n_casts + 1]
    kpad_ref, vpad_ref = rest[2 * n_casts + 1:]
    for src_ref, dst_ref in zip(src_refs, dst_refs):
        dst_ref[...] = src_ref[...].astype(BF16)
    _mix_out_body(sink_ref, outa_ref, q_ref, k_ref, v_ref, bias_ref, wout_ref,
                  o_ref, kpad_ref, vpad_ref, layer=layer, n_blocks=n_blocks)


def _mix_out_body(sink_ref, outa_ref, q_ref, k_ref, v_ref, bias_ref, wout_ref,
                  o_ref, kpad_ref, vpad_ref, *, layer, n_blocks):
    j = pl.program_id(1)
    seq = k_ref.shape[0]

    @pl.when(j == 0)
    def _():
        low = lax.broadcasted_iota(jnp.int32, (CHUNK, LANES), 1) < HEAD_DIM
        ones_low = jnp.where(low, 1.0, 0.0).astype(BF16)
        ones_high = jnp.where(low, 0.0, 1.0).astype(BF16)
        kpad_ref[0:CHUNK, :] = jnp.zeros((CHUNK, N_VARIANTS * LANES), BF16)
        kpad_ref[CHUNK + seq:, :] = jnp.zeros((CHUNK, N_VARIANTS * LANES), BF16)
        k_variants = _lane_half_variants(k_ref[...])
        v_variants = _lane_half_variants(v_ref[...])
        for vi in range(N_VARIANTS):
            kpad_ref[CHUNK:CHUNK + seq, vi * LANES:(vi + 1) * LANES] = k_variants[vi]
            c0 = 2 * vi * LANES
            vpad_ref[0:CHUNK, c0:c0 + LANES] = jnp.zeros((CHUNK, LANES), BF16)
            vpad_ref[CHUNK:CHUNK + seq, c0:c0 + LANES] = v_variants[vi]
            vpad_ref[CHUNK + seq:, c0:c0 + LANES] = jnp.zeros((CHUNK, LANES), BF16)
            ones = ones_low if vi % 2 == 0 else ones_high
            for r0 in range(0, seq + 2 * CHUNK, CHUNK):
                vpad_ref[r0:r0 + CHUNK, c0 + LANES:c0 + 2 * LANES] = ones

    blocks_per_step = o_ref.shape[0] // CHUNK
    first_rows = lax.broadcasted_iota(jnp.int32, (2 * CHUNK, 1), 0) < CHUNK
    low_lanes = lax.broadcasted_iota(jnp.int32, (2 * CHUNK, LANES), 1) < HEAD_DIM
    sink_cols = [[jnp.where(first_rows, sink_ref[layer, 4 * kvh + half], sink_ref[layer, 4 * kvh + half + 2])
                  for half in range(2)] for kvh in range(N_KV_HEADS)]
    sink_lanes = [jnp.where(low_lanes, cols[0], cols[1]) for cols in sink_cols]

    def scores(i, kvh):
        n = j * blocks_per_step + i
        edge = jnp.where(n == 0, 0, jnp.where(n == n_blocks - 1, 2, 1))
        win = pl.ds(pl.multiple_of(n * CHUNK, CHUNK), HALF_W)
        r = slice(i * CHUNK, (i + 1) * CHUNK)
        cg = 2 * kvh
        lhs = jnp.concatenate([q_ref[r, cg * LANES:(cg + 1) * LANES],
                               q_ref[r, (cg + 1) * LANES:(cg + 2) * LANES]], axis=0)
        keys = jnp.concatenate([kpad_ref[win, kvh * KV_COLS:kvh * KV_COLS + LANES],
                                kpad_ref[win, kvh * KV_COLS + LANES:(kvh + 1) * KV_COLS]], axis=0)
        s = lax.dot_general(lhs, keys, (((1,), (1,)), ((), ())), preferred_element_type=F32)
        s = s + bias_ref[edge, kvh]
        m = [jnp.maximum(jnp.max(s[:, h * HALF_W:(h + 1) * HALF_W], axis=-1, keepdims=True),
                         sink_cols[kvh][h]) for h in range(2)]
        return s, m, win

    def outputs(kvh, s, m, win):
        e = jnp.concatenate([jnp.exp2(s[:, h * HALF_W:(h + 1) * HALF_W] - m[h]) for h in range(2)],
                            axis=-1).astype(BF16)
        vals = jnp.concatenate([vpad_ref[win, 2 * kvh * KV_COLS:(2 * kvh + 1) * KV_COLS],
                                vpad_ref[win, (2 * kvh + 1) * KV_COLS:(2 * kvh + 2) * KV_COLS]], axis=0)
        pv = jnp.dot(e, vals, preferred_element_type=F32)
        e_sink = jnp.exp2(sink_lanes[kvh] - jnp.where(low_lanes, m[0], m[1]))
        o_kv = pv[:, :LANES] * (1.0 / (pv[:, LANES:] + e_sink))
        return [o_kv[0:CHUNK].astype(BF16), o_kv[CHUNK:].astype(BF16)]

    work = [(i, kvh) for i in range(blocks_per_step) for kvh in range(N_KV_HEADS)]
    per_block = 2 * N_KV_HEADS
    assert sum(OUT_PROJ_GROUPS) == blocks_per_step
    group_ends = set(np.cumsum(OUT_PROJ_GROUPS).tolist())
    tiles = []
    proj_pieces = []

    def proj_piece(rows, mixed, c):
        cols = slice(c * MXU_DIM, (c + 1) * MXU_DIM)
        o_ref[rows, cols] = jnp.dot(mixed, wout_ref[:, cols], preferred_element_type=F32)

    pending = scores(*work[0])
    for idx, (i, kvh) in enumerate(work):
        nxt = scores(*work[idx + 1]) if idx + 1 < len(work) else None
        tiles += outputs(kvh, *pending)
        pending = nxt
        if proj_pieces:
            proj_piece(*proj_pieces.pop(0))
        if kvh == N_KV_HEADS - 1 and (i + 1) in group_ends:
            n_group = len(tiles) // per_block
            rows = slice((i + 1 - n_group) * CHUNK, (i + 1) * CHUNK)
            attn = jnp.concatenate(
                [jnp.concatenate(tiles[b * per_block:(b + 1) * per_block], axis=-1)
                 for b in range(n_group)], axis=0)
            tiles = []
            mixed = jnp.concatenate([outa_ref[rows, :], attn], axis=-1)
            proj_pieces += [(rows, mixed, c) for c in range(o_ref.shape[1] // MXU_DIM)]
    for piece in proj_pieces:
        proj_piece(*piece)


def _mix_out(outa, q, k, v, sink, bias_tab, w_out, layer, batch, casts=()):
    t, d = outa.shape[0], w_out.shape[1]
    seq = t // batch
    rows = MIX_OUT_ROWS
    steps = seq // rows
    resident = pl.Buffered(1)
    row_spec = lambda w: pl.BlockSpec((rows, w), lambda b, j: (b * steps + j, 0))
    seq_spec = pl.BlockSpec((seq, KV_W), lambda b, j: (b, 0))
    cast_in_specs, cast_out_specs, cast_out_shapes = [], [], []
    for w, l in casts:
        _, w_rows, cols = w.shape
        slab = w_rows // (batch * steps)
        assert slab * batch * steps == w_rows and slab % BF16_SUBLANES == 0
        cast_in_specs.append(pl.BlockSpec((None, slab, cols), lambda b, j, l=l: (l, b * steps + j, 0)))
        cast_out_specs.append(pl.BlockSpec((slab, cols), lambda b, j: (b * steps + j, 0)))
        cast_out_shapes.append(jax.ShapeDtypeStruct((w_rows, cols), BF16))
    outs = pl.pallas_call(
        functools.partial(_mix_out_kernel, layer=layer, n_blocks=seq // CHUNK, n_casts=len(casts)),
        grid=(batch, steps),
        in_specs=[
            pl.BlockSpec(memory_space=pltpu.SMEM),
            row_spec(MIX_A), row_spec(MIX_B), seq_spec, seq_spec,
            pl.BlockSpec(bias_tab.shape, lambda b, j: (0, 0, 0, 0), pipeline_mode=resident),
            pl.BlockSpec((MIX_A + MIX_B, d), lambda b, j: (0, 0), pipeline_mode=resident),
        ] + cast_in_specs,
        out_specs=[row_spec(d)] + cast_out_specs,
        out_shape=[jax.ShapeDtypeStruct((t, d), F32)] + cast_out_shapes,
        scratch_shapes=[
            pltpu.VMEM((seq + 2 * CHUNK, N_VARIANTS * LANES), BF16),
            pltpu.VMEM((seq + 2 * CHUNK, 2 * N_VARIANTS * LANES), BF16),
        ],
        compiler_params=pltpu.CompilerParams(
            dimension_semantics=("parallel", "arbitrary"), vmem_limit_bytes=VMEM_LIMIT_BYTES),
        name="mix_out",
    )(sink, outa, q, k, v, bias_tab, w_out, *[w for w, _ in casts])
    return outs[0], outs[1:]


def _t5_bucket(rel):
    nb = N_BUCKETS // 2
    ret = (rel > 0).astype(np.int32) * nb
    n = np.abs(rel)
    max_exact = nb // 2
    large = max_exact + (np.log(np.maximum(n, 1).astype(np.float32) / max_exact)
                         / math.log(MAX_DISTANCE / max_exact) * (nb - max_exact)).astype(np.int32)
    large = np.minimum(large, nb - 1)
    return ret + np.where(n < max_exact, n, large).astype(np.int32)


def _bias_table(rel_bias):
    rel = np.arange(3 * CHUNK)[None, :] - CHUNK - np.arange(CHUNK)[:, None]
    bucket = lax.optimization_barrier(jnp.asarray(_t5_bucket(rel), jnp.int32))
    rb = rel_bias.astype(F32) * LOG2E
    bias = jnp.zeros((N_Q_HEADS,) + rel.shape, F32)
    for b in range(N_BUCKETS):
        bias = jnp.where((bucket == b)[None], rb[b][:, None, None], bias)
    band = np.abs(rel) <= CHUNK
    col = np.arange(3 * CHUNK)[None, :]
    edge_masks = [band & (col >= CHUNK), band, band & (col < 2 * CHUNK)]
    allowed = lax.optimization_barrier(
        jnp.asarray(sum(m.astype(np.int32) << e for e, m in enumerate(edge_masks)), jnp.int32))
    allowed = jnp.concatenate([allowed, allowed], axis=0)
    head_rows = np.array([[4 * (vi // 2) + (vi % 2), 4 * (vi // 2) + (vi % 2) + 2]
                          for vi in range(N_VARIANTS)])
    per_variant = jnp.stack([jnp.concatenate([bias[a], bias[b]], axis=0) for a, b in head_rows])
    tabs = [jnp.where(((allowed >> e) & 1)[None] == 1, per_variant, NEG_INF)
            for e in range(len(edge_masks))]
    tab = jnp.stack(tabs, axis=0)
    return jnp.concatenate([tab[:, 0::2], tab[:, 1::2]], axis=-1)


def kernel(x, ffn1_norm, ffn1_w_in, ffn1_w_out, mix_norm, w_mix_in, sgu_norm, sgu_w, sgu_b,
           q_norm, k_norm, sink, rel_bias, w_mix_out, ffn2_norm, ffn2_w_in, ffn2_w_out):
    batch, seq, d = x.shape
    depth = ffn1_norm.shape[0]
    assert seq % MIX_OUT_ROWS == 0 and (batch * seq) % FFN_ROWS == 0

    f32 = lambda a: a.astype(F32)
    ffn1_g, mix_g, ffn2_g, sgu_g = f32(ffn1_norm), f32(mix_norm), f32(ffn2_norm), f32(sgu_norm)
    q_g = jnp.tile(f32(q_norm), (1, N_Q_HEADS))
    k_g = jnp.tile(f32(k_norm), (1, N_KV_HEADS))
    sgu_b3 = jnp.broadcast_to(sgu_b.astype(F32)[..., None], sgu_b.shape + (LANES,))
    bias_tab = _bias_table(rel_bias)
    sink = sink.astype(F32) * LOG2E
    wi, wo = ffn1_w_in[0].astype(BF16), ffn1_w_out[0].astype(BF16)

    xt = x.reshape(batch * seq, d)
    for l in range(depth):
        xt, (mix_wi, mix_wo, wi, wo) = _ffn(
            xt, ffn1_g, wi, wo, l, [(w_mix_in, l), (w_mix_out, l), (ffn2_w_in, l), (ffn2_w_out, l)])
        outa, q, k, v = _mix_in(xt, mix_g, mix_wi, sgu_g, sgu_w, sgu_b3, q_g, k_g, l)
        nxt = [(ffn1_w_in, l + 1), (ffn1_w_out, l + 1)] if l + 1 < depth else []
        mixed, nxt_w = _mix_out(outa, q, k, v, sink, bias_tab, mix_wo, l, batch, nxt)
        xt, _ = _ffn(xt, ffn2_g, wi, wo, l, delta=mixed)
        if nxt:
            wi, wo = nxt_w
    return xt.reshape(batch, seq, d)
```
